```python
import math
import jax, jax.numpy as jnp
from jax import lax
import numpy as np

D_MODEL = 2048
BATCH = 16
SEQ = 256
DEPTH = 4
DEC_BATCH = 4
DEC_SEQ = 2048
PAST_LEN = 512

GRID_W = 64
N_BRANCH = 4
BRANCH_W = 512
RET_HEADS = 4
RET_DK = 128
RET_DV = 128
RET_CHUNK = 128
DIFF_HEADS = 4
DIFF_DH = 64
DIFF_DV = 2 * DIFF_DH
CONV_W = 3
FNET_GROUPS = 4
FNET_GW = 128
N_EXPERTS = 32
TOP_K = 4
D_EXPERT = 2048
SWIGLU_ALPHA = 1.702
SWIGLU_LIMIT = 7.0
ROPE_BASE = 10000.0
Q_BLOCK = 128
LN_EPS = 1e-5
RMS_EPS = 1e-6
DEEP_ALPHA = (2 * DEPTH) ** 0.25
DEEP_BETA = (8 * DEPTH) ** -0.25

IN_SIZES = (RET_HEADS * RET_DK, RET_HEADS * RET_DK, RET_HEADS * RET_DV, RET_HEADS * RET_DV,
            DIFF_HEADS * 2 * DIFF_DH, DIFF_HEADS * 2 * DIFF_DH, DIFF_HEADS * DIFF_DV,
            BRANCH_W, BRANCH_W, BRANCH_W, FNET_GROUPS * FNET_GW)
IN_SPLITS = tuple(int(s) for s in np.cumsum(IN_SIZES)[:-1])
D_IN = int(sum(IN_SIZES))

kernel_name = 'hybrid_diffusion_parallel_mixer_step'


def layer_norm(x):
    xf = x.astype(jnp.float32)
    xc = xf - jnp.mean(xf, -1, keepdims=True)
    return xc * lax.rsqrt(jnp.mean(xc * xc, -1, keepdims=True) + LN_EPS)


def layer_norm_affine(x, g, b):
    return (layer_norm(x) * g + b).astype(x.dtype)


def rms_norm(x):
    xf = x.astype(jnp.float32)
    return xf * lax.rsqrt(jnp.mean(xf * xf, -1, keepdims=True) + RMS_EPS)


def grid_angles(n_tokens, head_dim):
    rows = n_tokens // GRID_W
    row = jnp.repeat(jnp.arange(rows, dtype=jnp.float32), GRID_W)
    col = jnp.tile(jnp.arange(GRID_W, dtype=jnp.float32), rows)
    axis_dim = head_dim // 2
    inv_freq = ROPE_BASE ** (-jnp.arange(0, axis_dim, 2, dtype=jnp.float32) / axis_dim)
    return row[:, None] * inv_freq, col[:, None] * inv_freq


def rotate_half(x, ang):
    x1, x2 = jnp.split(x, 2, axis=-1)
    cos, sin = jnp.cos(ang), jnp.sin(ang)
    return jnp.concatenate([x1 * cos - x2 * sin, x1 * sin + x2 * cos], axis=-1)


def axial_rope(x):
    n_tokens, d = x.shape[1], x.shape[-1]
    ang_r, ang_c = grid_angles(n_tokens, d)
    bshape = (n_tokens,) + (1,) * (x.ndim - 3) + (ang_r.shape[-1],)
    xr, xc = jnp.split(x.astype(jnp.float32), 2, axis=-1)
    out = jnp.concatenate([rotate_half(xr, ang_r.reshape(bshape)),
                           rotate_half(xc, ang_c.reshape(bshape))], axis=-1)
    return out.astype(x.dtype)


def retention_scan(q, k, v, log_gamma, s0):
    bsz, n_tok, n_h, dk = q.shape
    dv = v.shape[-1]
    n_chunk = n_tok // RET_CHUNK
    qc = q.astype(jnp.float32).reshape(bsz, n_chunk, RET_CHUNK, n_h, dk)
    kc = k.astype(jnp.float32).reshape(bsz, n_chunk, RET_CHUNK, n_h, dk)
    vc = v.astype(jnp.float32).reshape(bsz, n_chunk, RET_CHUNK, n_h, dv)
    pos = jnp.arange(RET_CHUNK, dtype=jnp.float32)
    dist = pos[:, None] - pos[None, :]
    decay = jnp.exp(jnp.where(dist[None] >= 0, dist[None] * log_gamma[:, None, None], -jnp.inf))
    scores = jnp.einsum('bnihd,bnjhd->bnhij', qc, kc) * decay[None, None]
    o_intra = jnp.einsum('bnhij,bnjhe->bnihe', scores, vc)
    k_w = kc * jnp.exp((RET_CHUNK - 1 - pos)[:, None] * log_gamma[None, :])[None, None, :, :, None]
    kv = jnp.einsum('bnjhd,bnjhe->nbhde', k_w, vc)
    chunk_decay = jnp.exp(RET_CHUNK * log_gamma)[None, :, None, None]

    def step(s, kv_n):
        return chunk_decay * s + kv_n, s

    s_final, s_prev = lax.scan(step, s0.astype(jnp.float32), kv)
    q_w = qc * jnp.exp((pos + 1)[:, None] * log_gamma[None, :])[None, None, :, :, None]
    o_cross = jnp.einsum('bnihd,nbhde->bnihe', q_w, s_prev)
    return (o_intra + o_cross).reshape(bsz, n_tok, n_h, dv), s_final


def bidir_retention(q, k, v, log_gamma, s0_f, s0_b):
    o_f, s_f = retention_scan(q, k, v, log_gamma[0], s0_f)
    o_b, s_b = retention_scan(q[:, ::-1], k[:, ::-1], v[:, ::-1], log_gamma[1], s0_b)
    return o_f + o_b[:, ::-1], s_f, s_b


def diff_attention(q, k, v, lam):
    bsz, n_q, n_h, _, dh = q.shape
    dv = v.shape[-1]
    qb = q.reshape(bsz, n_q // Q_BLOCK, Q_BLOCK, n_h, 2, dh).swapaxes(0, 1)
    scale = dh ** -0.5

    def one_block(q_blk):
        s = jnp.einsum('bqhmd,bkhmd->bhmqk', q_blk, k).astype(jnp.float32) * scale
        p = jax.nn.softmax(s, axis=-1)
        w = p[:, :, 0] - lam * p[:, :, 1]
        return jnp.einsum('bhqk,bkhe->bqhe', w.astype(v.dtype), v)

    o = lax.map(one_block, qb)
    return o.swapaxes(0, 1).reshape(bsz, n_q, n_h, dv)


def mixer(h, lp, lam_init, ctx):
    bsz, n_tok, _ = h.shape
    proj = jnp.einsum('bld,de->ble', h, lp['w_in'])
    rq, rk, rv, rg, dq, dk, dv, cb, cc, cx, fx = jnp.split(proj, IN_SPLITS, axis=-1)
    rq = rq.reshape(bsz, n_tok, RET_HEADS, RET_DK)
    rk = rk.reshape(bsz, n_tok, RET_HEADS, RET_DK) * (RET_DK ** -0.5)
    rv = rv.reshape(bsz, n_tok, RET_HEADS, RET_DV)
    dq = dq.reshape(bsz, n_tok, DIFF_HEADS, 2, DIFF_DH)
    dk = dk.reshape(bsz, n_tok, DIFF_HEADS, 2, DIFF_DH)
    dv = dv.reshape(bsz, n_tok, DIFF_HEADS, DIFF_DV)
    log_gamma = jax.nn.log_sigmoid(lp['ret_decay_logit'].astype(jnp.float32))
    if ctx is None:
        s0_f = s0_b = jnp.zeros((bsz, RET_HEADS, RET_DK, RET_DV), jnp.float32)
        keys, vals = dk, dv
    else:
        k_ctx, v_ctx, s0_f, s0_b = ctx
        rq, rk, dq = axial_rope(rq), axial_rope(rk), axial_rope(dq)
        keys = jnp.concatenate([axial_rope(dk), k_ctx.astype(dk.dtype)], axis=1)
        vals = jnp.concatenate([dv, v_ctx.astype(dv.dtype)], axis=1)
    o_ret, s_f, s_b = bidir_retention(rq, rk, rv, log_gamma, s0_f, s0_b)
    ret_out = jax.nn.silu(rg) * rms_norm(o_ret).astype(h.dtype).reshape(bsz, n_tok, BRANCH_W)
    lam_v = lp['diff_lambda'].astype(jnp.float32)
    lam = jnp.exp(jnp.sum(lam_v[0] * lam_v[1])) - jnp.exp(jnp.sum(lam_v[2] * lam_v[3])) + lam_init
    o_diff = diff_attention(dq, keys, vals, lam)
    diff_out = (rms_norm(o_diff) * (1.0 - lam_init)).astype(h.dtype).reshape(bsz, n_tok, BRANCH_W)
    u = jnp.pad(cc * cx, ((0, 0), (1, 1), (0, 0)))
    cw = lp['conv_w']
    conv = u[:, :-2] * cw[:, 0] + u[:, 1:-1] * cw[:, 1] + u[:, 2:] * cw[:, 2]
    conv_out = cb * conv
    fg = fx.reshape(bsz, n_tok, FNET_GROUPS, FNET_GW).astype(jnp.float32)
    f_out = jnp.fft.fft2(fg, axes=(1, 3), norm='ortho').real.astype(h.dtype).reshape(bsz, n_tok, BRANCH_W)
    branches = jnp.stack([ret_out, diff_out, conv_out, f_out], axis=2)
    up = jnp.einsum('blnw,nwd->blnd', branches, lp['w_branch'])
    gates = jax.nn.sigmoid(jnp.einsum('bld,de->ble', h, lp['w_mgate']) + lp['b_mgate'])
    merged = jnp.einsum('blnd,blnd->bld', gates.reshape(bsz, n_tok, N_BRANCH, D_MODEL), up)
    out = jnp.einsum('bld,de->ble', merged, lp['w_out'])
    new_ctx = (dk, dv, s_f, s_b) if ctx is None else None
    return out, new_ctx


def moe(h, lp):
    bsz, n_tok, d = h.shape
    x = h.reshape(bsz * n_tok, d)
    logits = (x @ lp['w_router'] + lp['b_router']).astype(jnp.float32)
    top_v, top_i = lax.top_k(logits, TOP_K)
    top_w = jax.nn.softmax(top_v, axis=-1)
    comb = jnp.sum(jax.nn.one_hot(top_i, N_EXPERTS, dtype=jnp.float32) * top_w[..., None], axis=1)
    comb = comb.astype(x.dtype)

    def expert_step(acc, ew):
        w_gu, b_gu, w_dn, b_dn, cw = ew
        gate, lin = jnp.split(x @ w_gu + b_gu, 2, axis=-1)
        gate = jnp.minimum(gate, SWIGLU_LIMIT)
        lin = jnp.clip(lin, -SWIGLU_LIMIT, SWIGLU_LIMIT)
        y = ((lin + 1.0) * gate * jax.nn.sigmoid(SWIGLU_ALPHA * gate)) @ w_dn + b_dn
        return acc + cw[:, None] * y, None

    acc, _ = lax.scan(expert_step, jnp.zeros_like(x),
                      (lp['w_gu'], lp['b_gu'], lp['w_down'], lp['b_down'], comb.T))
    return acc.reshape(bsz, n_tok, d)


def trunk_layer(x, cond, lp, lam_init, ctx):
    mod = jnp.einsum('bd,de->be', jax.nn.silu(cond), lp['w_mod']) + lp['b_mod']
    sh1, sc1, g1, sh2, sc2, g2 = jnp.split(mod[:, None, :], 6, axis=-1)
    h = (layer_norm(x) * (1.0 + sc1) + sh1).astype(x.dtype)
    mix, new_ctx = mixer(h, lp, lam_init, ctx)
    x = layer_norm_affine(DEEP_ALPHA * x + g1 * mix, lp['ln1_g'], lp['ln1_b'])
    h = (layer_norm(x) * (1.0 + sc2) + sh2).astype(x.dtype)
    x = layer_norm_affine(DEEP_ALPHA * x + g2 * moe(h, lp), lp['ln2_g'], lp['ln2_b'])
    return x, new_ctx


def setup_inputs(seed: int = 0) -> dict:
    key = jax.random.key(seed)
    ks = jax.random.split(key, 32)
    f32 = jnp.float32

    def nrm(k, shape, scale):
        return jax.random.normal(k, shape, f32) * scale

    gamma = 1.0 - 2.0 ** (-5.0 - np.arange(RET_HEADS))
    base_logit = jnp.asarray(np.log(gamma / (1.0 - gamma)), f32)
    return {
        'x_prompt': nrm(ks[0], (BATCH, SEQ, D_MODEL), 1.0),
        'x_sample': nrm(ks[1], (DEC_BATCH, DEC_SEQ, D_MODEL), 1.0),
        'c': nrm(ks[2], (DEC_BATCH, D_MODEL), 1.0),
        'cache_k': nrm(ks[3], (DEC_BATCH, DEPTH, PAST_LEN, DIFF_HEADS, 2, DIFF_DH), 1.0),
        'cache_v': nrm(ks[4], (DEC_BATCH, DEPTH, PAST_LEN, DIFF_HEADS, DIFF_DV), 1.0),
        'state_ret': nrm(ks[5], (DEC_BATCH, DEPTH, 2, RET_HEADS, RET_DK, RET_DV), 1.0),
        'c_ctx': nrm(ks[6], (D_MODEL,), 1.0),
        'w_mod': nrm(ks[7], (DEPTH, D_MODEL, 6 * D_MODEL), D_MODEL ** -0.5),
        'b_mod': nrm(ks[8], (DEPTH, 6 * D_MODEL), 0.02),
        'w_in': nrm(ks[9], (DEPTH, D_MODEL, D_IN), D_MODEL ** -0.5),
        'ret_decay_logit': base_logit + nrm(ks[10], (DEPTH, 2, RET_HEADS), 0.1),
        'diff_lambda': nrm(ks[11], (DEPTH, 4, DIFF_DH), 0.1),
        'conv_w': nrm(ks[12], (DEPTH, BRANCH_W, CONV_W), CONV_W ** -0.5),
        'w_branch': nrm(ks[13], (DEPTH, N_BRANCH, BRANCH_W, D_MODEL), BRANCH_W ** -0.5 * DEEP_BETA),
        'w_mgate': nrm(ks[14], (DEPTH, D_MODEL, N_BRANCH * D_MODEL), D_MODEL ** -0.5),
        'b_mgate': nrm(ks[15], (DEPTH, N_BRANCH * D_MODEL), 0.02),
        'w_out': nrm(ks[16], (DEPTH, D_MODEL, D_MODEL), D_MODEL ** -0.5 * DEEP_BETA),
        'ln1_g': 1.0 + nrm(ks[17], (DEPTH, D_MODEL), 0.02),
        'ln1_b': nrm(ks[18], (DEPTH, D_MODEL), 0.02),
        'ln2_g': 1.0 + nrm(ks[19], (DEPTH, D_MODEL), 0.02),
        'ln2_b': nrm(ks[20], (DEPTH, D_MODEL), 0.02),
        'w_router': nrm(ks[21], (DEPTH, D_MODEL, N_EXPERTS), D_MODEL ** -0.5),
        'b_router': nrm(ks[22], (DEPTH, N_EXPERTS), 0.01),
        'w_gu': nrm(ks[23], (DEPTH, N_EXPERTS, D_MODEL, 2 * D_EXPERT), D_MODEL ** -0.5),
        'b_gu': nrm(ks[24], (DEPTH, N_EXPERTS, 2 * D_EXPERT), 0.02),
        'w_down': nrm(ks[25], (DEPTH, N_EXPERTS, D_EXPERT, D_MODEL), D_EXPERT ** -0.5 * DEEP_BETA),
        'b_down': nrm(ks[26], (DEPTH, N_EXPERTS, D_MODEL), 0.02),
    }


def reference(x_prompt, x_sample, c, cache_k, cache_v, state_ret, c_ctx, w_mod, b_mod, w_in,
              ret_decay_logit, diff_lambda, conv_w, w_branch, w_mgate, b_mgate, w_out,
              ln1_g, ln1_b, ln2_g, ln2_b, w_router, b_router, w_gu, b_gu, w_down, b_down):
    y_p, y_s = x_prompt, x_sample
    cond_ctx = c_ctx[None, :]
    ks_out, vs_out, ss_out = [], [], []
    for l in range(DEPTH):
        lp = dict(w_mod=w_mod[l], b_mod=b_mod[l], w_in=w_in[l], ret_decay_logit=ret_decay_logit[l],
                  diff_lambda=diff_lambda[l], conv_w=conv_w[l], w_branch=w_branch[l],
                  w_mgate=w_mgate[l], b_mgate=b_mgate[l], w_out=w_out[l],
                  ln1_g=ln1_g[l], ln1_b=ln1_b[l], ln2_g=ln2_g[l], ln2_b=ln2_b[l],
                  w_router=w_router[l], b_router=b_router[l], w_gu=w_gu[l], b_gu=b_gu[l],
                  w_down=w_down[l], b_down=b_down[l])
        lam_init = 0.8 - 0.6 * math.exp(-0.3 * l)
        y_p, (k_l, v_l, s_f, s_b) = trunk_layer(y_p, cond_ctx, lp, lam_init, None)
        ks_out.append(k_l)
        vs_out.append(v_l)
        ss_out.append(jnp.stack([s_f, s_b], axis=1))
        ctx_l = (cache_k[:, l], cache_v[:, l], state_ret[:, l, 0], state_ret[:, l, 1])
        y_s, _ = trunk_layer(y_s, c, lp, lam_init, ctx_l)
    new_cache_k = jnp.stack(ks_out, axis=1)
    new_cache_v = jnp.stack(vs_out, axis=1)
    new_state_ret = jnp.stack(ss_out, axis=1)
    return (y_p, y_s, new_cache_k, new_cache_v, new_state_ret)
```

```python
import functools
import math

import jax
import jax.numpy as jnp
import numpy as np
from jax import lax
from jax.experimental import pallas as pl
from jax.experimental.pallas import tpu as pltpu

D_MODEL = 2048
BATCH = 16
SEQ = 256
DEPTH = 4
DEC_BATCH = 4
DEC_SEQ = 2048
PAST_LEN = 512
GRID_W = 64
N_BRANCH = 4
BRANCH_W = 512
RET_HEADS = 4
RET_DK = 128
RET_CHUNK = 128
DIFF_HEADS = 4
DIFF_DH = 64
N_EXPERTS = 32
TOP_K = 4
D_EXPERT = 2048
SWIGLU_ALPHA = 1.702
SWIGLU_LIMIT = 7.0
ROPE_BASE = 10000.0
LN_EPS = 1e-5
RMS_EPS = 1e-6
DEEP_ALPHA = (2 * DEPTH) ** 0.25
D_IN = 5632

LANE = 128
T_PROMPT = BATCH * SEQ
T_SAMPLE = DEC_BATCH * DEC_SEQ
T_ALL = T_PROMPT + T_SAMPLE
N_COND = 1 + DEC_BATCH
COND_PAD = 8

ROW_TILE = 256
MM_TM = 512
MM_TN = 512
MERGE_TN = 256
MOE_TM = 256
MOE_TF = 512
MOE_TN = 512
CAST_ROWS = 256
P_PAIRS = T_ALL * TOP_K
P_PAD = P_PAIRS + N_EXPERTS * MOE_TM
N_MOE_TILES = P_PAD // MOE_TM
VMEM_LIMIT = 56 * 1024 * 1024

COL_RQ, COL_RK, COL_RV, COL_RG = 0, 4, 8, 12
COL_DQ, COL_DK, COL_DV = 16, 20, 24
COL_CB, COL_CC, COL_CX = 28, 32, 36
COL_FX = 40

BF16 = jnp.bfloat16
F32 = jnp.float32


def _params(*sem):
    return pltpu.CompilerParams(dimension_semantics=sem, vmem_limit_bytes=VMEM_LIMIT)


def _cond_row_of_tile(i, tile):
    start = i * tile
    return jnp.where(start < T_PROMPT, 0, 1 + (start - T_PROMPT) // DEC_SEQ)


def _cache_bf16(src_ref, dst_ref):
    n = src_ref.shape[0] // CAST_ROWS

    def body(i, carry):
        r = pl.multiple_of(i * CAST_ROWS, CAST_ROWS)
        dst_ref[pl.ds(r, CAST_ROWS), :] = src_ref[pl.ds(r, CAST_ROWS), :].astype(BF16)
        return carry

    lax.fori_loop(0, n, body, 0)


def _mod_kernel(c_ref, w_ref, b_ref, o_ref):
    c = c_ref[...]
    a = (c * jax.nn.sigmoid(c)).astype(BF16)
    o_ref[...] = jnp.dot(a, w_ref[...].astype(BF16), preferred_element_type=F32) + b_ref[...]


def _modulation(cond_pad, w_mod, b_mod, layer):
    n = 6 * D_MODEL
    return pl.pallas_call(
        _mod_kernel,
        out_shape=jax.ShapeDtypeStruct((COND_PAD, n), F32),
        grid=(n // MM_TN,),
        in_specs=[
            pl.BlockSpec((COND_PAD, D_MODEL), lambda j: (0, 0)),
            pl.BlockSpec((None, D_MODEL, MM_TN), lambda j: (layer, 0, j)),
            pl.BlockSpec((None, 1, MM_TN), lambda j: (layer, 0, j)),
        ],
        out_specs=pl.BlockSpec((COND_PAD, MM_TN), lambda j: (0, j)),
        compiler_params=_params("arbitrary"),
        name="modulation",
    )(cond_pad, w_mod, b_mod.reshape(DEPTH, 1, n))


def _ln(x):
    xc = x - jnp.mean(x, axis=-1, keepdims=True)
    return xc * lax.rsqrt(jnp.mean(xc * xc, axis=-1, keepdims=True) + LN_EPS)


def _ln_mod_kernel(x_ref, sc_ref, sh_ref, h_ref):
    h_ref[...] = (_ln(x_ref[...]) * (1.0 + sc_ref[...]) + sh_ref[...]).astype(h_ref.dtype)


def _mod_spec(which):
    return pl.BlockSpec((None, None, 1, D_MODEL),
                        lambda i: (_cond_row_of_tile(i, ROW_TILE), which, 0, 0))


def _ln_mod(x, mod4, sc_idx, sh_idx):
    return pl.pallas_call(
        _ln_mod_kernel,
        out_shape=jax.ShapeDtypeStruct((T_ALL, D_MODEL), BF16),
        grid=(T_ALL // ROW_TILE,),
        in_specs=[pl.BlockSpec((ROW_TILE, D_MODEL), lambda i: (i, 0)),
                  _mod_spec(sc_idx), _mod_spec(sh_idx)],
        out_specs=pl.BlockSpec((ROW_TILE, D_MODEL), lambda i: (i, 0)),
        compiler_params=_params("parallel"),
        name="ln_mod",
    )(x, mod4, mod4)


def _resid_ln_kernel(x_ref, y_ref, g_ref, lg_ref, lb_ref, sc_ref, sh_ref, xo_ref, h_ref):
    z = DEEP_ALPHA * x_ref[...] + g_ref[...] * y_ref[...]
    xn = _ln(z) * lg_ref[...] + lb_ref[...]
    xo_ref[...] = xn
    h_ref[...] = (_ln(xn) * (1.0 + sc_ref[...]) + sh_ref[...]).astype(h_ref.dtype)


def _resid_ln(x, y, mod_gate, gate_idx, ln_g, ln_b, layer, mod_next, sc_idx, sh_idx):
    row = pl.BlockSpec((ROW_TILE, D_MODEL), lambda i: (i, 0))
    vec = pl.BlockSpec((None, 1, D_MODEL), lambda i: (layer, 0, 0))
    return pl.pallas_call(
        _resid_ln_kernel,
        out_shape=(jax.ShapeDtypeStruct((T_ALL, D_MODEL), F32),
                   jax.ShapeDtypeStruct((T_ALL, D_MODEL), BF16)),
        grid=(T_ALL // ROW_TILE,),
        in_specs=[row, row, _mod_spec(gate_idx), vec, vec, _mod_spec(sc_idx), _mod_spec(sh_idx)],
        out_specs=(row, row),
        compiler_params=_params("parallel"),
        name="resid_ln",
    )(x, y, mod_gate, ln_g.reshape(DEPTH, 1, D_MODEL), ln_b.reshape(DEPTH, 1, D_MODEL),
      mod_next, mod_next)


def _mm_kernel(x_ref, w_ref, o_ref, wb_ref):
    @pl.when(pl.program_id(1) == 0)
    def _():
        _cache_bf16(w_ref, wb_ref)

    o_ref[...] = jnp.dot(x_ref[...], wb_ref[...], preferred_element_type=F32).astype(o_ref.dtype)


def _matmul(x, w, layer, out_dtype, name):
    m, k = x.shape
    n = w.shape[-1]
    return pl.pallas_call(
        _mm_kernel,
        out_shape=jax.ShapeDtypeStruct((m, n), out_dtype),
        grid=(n // MM_TN, m // MM_TM),
        in_specs=[pl.BlockSpec((MM_TM, k), lambda j, i: (i, 0)),
                  pl.BlockSpec((None, k, MM_TN), lambda j, i: (layer, 0, j))],
        out_specs=pl.BlockSpec((MM_TM, MM_TN), lambda j, i: (i, j)),
        scratch_shapes=[pltpu.VMEM((k, MM_TN), BF16)],
        compiler_params=_params("arbitrary", "arbitrary"),
        name=name,
    )(x, w)


def _rope_tables(d):
    rows = DEC_SEQ // GRID_W
    row = jnp.repeat(jnp.arange(rows, dtype=F32), GRID_W)
    col = jnp.tile(jnp.arange(GRID_W, dtype=F32), rows)
    axis_dim = d // 2
    inv_freq = ROPE_BASE ** (-jnp.arange(0, axis_dim, 2, dtype=F32) / axis_dim)
    ang_r = row[:, None] * inv_freq
    ang_c = col[:, None] * inv_freq
    cos = jnp.concatenate([jnp.cos(ang_r), jnp.cos(ang_r), jnp.cos(ang_c), jnp.cos(ang_c)], -1)
    sin = jnp.concatenate([-jnp.sin(ang_r), jnp.sin(ang_r), -jnp.sin(ang_c), jnp.sin(ang_c)], -1)
    reps = LANE // d
    return jnp.tile(cos, (1, reps)), jnp.tile(sin, (1, reps))


def _rope(x, cos, sin, d):
    q = d // 4
    lane = lax.broadcasted_iota(jnp.int32, x.shape, 1)
    first = (lane % (2 * q)) < q
    partner = jnp.where(first, pltpu.roll(x, LANE - q, 1), pltpu.roll(x, q, 1))
    return x * cos + partner * sin


def _retention_kernel(*refs, seq, use_ctx):
    if use_ctx:
        (q_ref, k_ref, v_ref, g_ref, lgt_ref, s0_ref, cos_ref, sin_ref, o_ref,
         of_ref, ob_ref) = refs
    else:
        q_ref, k_ref, v_ref, g_ref, lgt_ref, o_ref, sfin_ref, of_ref, ob_ref = refs
    c = RET_CHUNK
    n_chunk = seq // c
    ii = lax.broadcasted_iota(jnp.int32, (c, c), 0).astype(F32)
    jj = lax.broadcasted_iota(jnp.int32, (c, c), 1).astype(F32)
    col_pos = lax.broadcasted_iota(jnp.int32, (c, 1), 0).astype(F32)
    row_pos = lax.broadcasted_iota(jnp.int32, (1, c), 1).astype(F32)

    def direction(d):
        logit = lgt_ref[d]
        lg = jnp.minimum(logit, 0.0) - jnp.log1p(jnp.exp(-jnp.abs(logit)))
        lg_mat = jnp.broadcast_to(lg[0:1, :], (c, c))
        lg_col = lg[:, 0:1][0:1, :]
        if d == 0:
            dist = ii - jj
            kw = jnp.exp((c - 1 - row_pos) * lg_col)
            qw = jnp.exp((col_pos + 1.0) * lg_col)
        else:
            dist = jj - ii
            kw = jnp.exp(row_pos * lg_col)
            qw = jnp.exp((c - col_pos) * lg_col)
        decay = jnp.where(dist >= 0, jnp.exp(jnp.where(dist >= 0, dist, 0.0) * lg_mat), 0.0)
        chunk_decay = jnp.exp(c * lg_mat)
        return decay, kw, qw, chunk_decay

    def chunk(n):
        rows = slice(n * c, (n + 1) * c)
        q = q_ref[rows, :]
        k = k_ref[rows, :] * (RET_DK ** -0.5)
        if use_ctx:
            q = _rope(q, cos_ref[rows, :], sin_ref[rows, :], RET_DK)
            k = _rope(k, cos_ref[rows, :], sin_ref[rows, :], RET_DK)
        return q, k.T, v_ref[rows, :].astype(BF16)

    for d, acc_ref in ((0, of_ref), (1, ob_ref)):
        decay, kw, qw, chunk_decay = direction(d)
        if use_ctx:
            state = s0_ref[d]
        else:
            state = jnp.zeros((RET_DK, RET_DK), F32)
        order = range(n_chunk) if d == 0 else range(n_chunk - 1, -1, -1)
        for n in order:
            q, kt, v = chunk(n)
            scores = jnp.dot(q.astype(BF16), kt.astype(BF16), preferred_element_type=F32) * decay
            o = jnp.dot(scores.astype(BF16), v, preferred_element_type=F32)
            o = o + jnp.dot((q * qw).astype(BF16), state.astype(BF16), preferred_element_type=F32)
            acc_ref[n * c:(n + 1) * c, :] = o
            state = chunk_decay * state + jnp.dot((kt * kw).astype(BF16), v,
                                                  preferred_element_type=F32)
        if not use_ctx:
            sfin_ref[d] = state

    o = of_ref[...] + ob_ref[...]
    o = o * lax.rsqrt(jnp.mean(o * o, axis=-1, keepdims=True) + RMS_EPS)
    g = g_ref[...]
    o_ref[...] = (g * jax.nn.sigmoid(g) * o).astype(o_ref.dtype)


def _retention(proj, decay_logit, layer, *, use_ctx, state_ret=None, tables=None):
    seq, nb, row0 = (DEC_SEQ, DEC_BATCH, T_PROMPT // DEC_SEQ) if use_ctx else (SEQ, BATCH, 0)

    def col(c0):
        return pl.BlockSpec((seq, LANE), lambda b, h: (row0 + b, c0 + h))

    lgt = jnp.broadcast_to(decay_logit[layer][:, :, None, None], (2, RET_HEADS, 8, LANE))
    in_specs = [col(COL_RQ), col(COL_RK), col(COL_RV), col(COL_RG),
                pl.BlockSpec((2, None, 8, LANE), lambda b, h: (0, h, 0, 0))]
    args = [proj, proj, proj, proj, lgt]
    out_branch = jax.ShapeDtypeStruct((nb * seq, BRANCH_W), BF16)
    branch_spec = pl.BlockSpec((seq, LANE), lambda b, h: (b, h))
    if use_ctx:
        in_specs += [pl.BlockSpec((None, None, 2, None, RET_DK, RET_DK),
                                  lambda b, h: (b, layer, 0, h, 0, 0)),
                     pl.BlockSpec((seq, LANE), lambda b, h: (0, 0)),
                     pl.BlockSpec((seq, LANE), lambda b, h: (0, 0))]
        args += [state_ret, tables[0], tables[1]]
        out_shape, out_specs = out_branch, branch_spec
    else:
        out_shape = (out_branch,
                     jax.ShapeDtypeStruct((nb, 2, RET_HEADS, RET_DK, RET_DK), F32))
        out_specs = (branch_spec,
                     pl.BlockSpec((None, 2, None, RET_DK, RET_DK), lambda b, h: (b, 0, h, 0, 0)))
    return pl.pallas_call(
        functools.partial(_retention_kernel, seq=seq, use_ctx=use_ctx),
        out_shape=out_shape,
        grid=(nb, RET_HEADS),
        in_specs=in_specs,
        out_specs=out_specs,
        scratch_shapes=[pltpu.VMEM((seq, LANE), F32), pltpu.VMEM((seq, LANE), F32)],
        compiler_params=_params("parallel", "parallel"),
        name="retention_ctx" if use_ctx else "retention",
    )(*args)


def _diff_attn_kernel(*refs, seq, tq, use_ctx, lam_init):
    if use_ctx:
        (q_ref, k_ref, v_ref, lam_ref, kc_ref, vc_ref, cosq_ref, sinq_ref, cosk_ref, sink_ref,
         o_ref, kt_ref, va_ref) = refs
    else:
        q_ref, k_ref, v_ref, lam_ref, o_ref, kt_ref, va_ref = refs
    c = LANE

    @pl.when(pl.program_id(2) == 0)
    def _():
        for n in range(seq // c):
            rows = slice(n * c, (n + 1) * c)
            k = k_ref[rows, :]
            if use_ctx:
                k = _rope(k, cosk_ref[rows, :], sink_ref[rows, :], DIFF_DH)
            kt_ref[:, rows] = k.T.astype(BF16)
        va_ref[0:seq, :] = v_ref[...].astype(BF16)
        if use_ctx:
            for n in range(PAST_LEN // c):
                rows = slice(n * c, (n + 1) * c)
                kt_ref[:, seq + n * c:seq + (n + 1) * c] = kc_ref[rows, :].T.astype(BF16)
            va_ref[seq:seq + PAST_LEN, :] = vc_ref[...].astype(BF16)

    lam_v = lam_ref[...]
    lam = (jnp.exp(jnp.sum(lam_v[0:1] * lam_v[1:2], axis=-1, keepdims=True))
           - jnp.exp(jnp.sum(lam_v[2:3] * lam_v[3:4], axis=-1, keepdims=True)) + lam_init)
    q = q_ref[...]
    if use_ctx:
        q = _rope(q, cosq_ref[...], sinq_ref[...], DIFF_DH)
    lane = lax.broadcasted_iota(jnp.int32, q.shape, 1)
    scale = DIFF_DH ** -0.5
    kt = kt_ref[...]
    probs = []
    for m in range(2):
        qm = jnp.where((lane < DIFF_DH) == (m == 0), q, 0.0).astype(BF16)
        s = jnp.dot(qm, kt, preferred_element_type=F32) * scale
        p = jnp.exp(s - jnp.max(s, axis=-1, keepdims=True))
        probs.append(p / jnp.sum(p, axis=-1, keepdims=True))
    w = probs[0] - lam * probs[1]
    o = jnp.dot(w.astype(BF16), va_ref[...], preferred_element_type=F32)
    o = o * lax.rsqrt(jnp.mean(o * o, axis=-1, keepdims=True) + RMS_EPS) * (1.0 - lam_init)
    o_ref[...] = o.astype(o_ref.dtype)


def _diff_attention(proj, diff_lambda, layer, lam_init, *, use_ctx, cache_k=None, cache_v=None,
                    tables=None):
    seq, nb, row0 = (DEC_SEQ, DEC_BATCH, T_PROMPT // DEC_SEQ) if use_ctx else (SEQ, BATCH, 0)
    tq = 256
    n_q = seq // tq
    n_keys = seq + (PAST_LEN if use_ctx else 0)
    in_specs = [pl.BlockSpec((tq, LANE), lambda b, h, i: ((row0 + b) * n_q + i, COL_DQ + h)),
                pl.BlockSpec((seq, LANE), lambda b, h, i: (row0 + b, COL_DK + h)),
                pl.BlockSpec((seq, LANE), lambda b, h, i: (row0 + b, COL_DV + h)),
                pl.BlockSpec((None, 4, DIFF_DH), lambda b, h, i: (layer, 0, 0))]
    args = [proj, proj, proj, diff_lambda]
    if use_ctx:
        ctx_spec = pl.BlockSpec((None, None, PAST_LEN, LANE), lambda b, h, i: (b, layer, 0, h))
        full = pl.BlockSpec((seq, LANE), lambda b, h, i: (0, 0))
        qtab = pl.BlockSpec((tq, LANE), lambda b, h, i: (i, 0))
        in_specs += [ctx_spec, ctx_spec, qtab, qtab, full, full]
        args += [cache_k.reshape(DEC_BATCH, DEPTH, PAST_LEN, DIFF_HEADS * 2 * DIFF_DH),
                 cache_v.reshape(DEC_BATCH, DEPTH, PAST_LEN, DIFF_HEADS * 2 * DIFF_DH),
                 tables[0], tables[1], tables[0], tables[1]]
    return pl.pallas_call(
        functools.partial(_diff_attn_kernel, seq=seq, tq=tq, use_ctx=use_ctx, lam_init=lam_init),
        out_shape=jax.ShapeDtypeStruct((nb * seq, BRANCH_W), BF16),
        grid=(nb, DIFF_HEADS, n_q),
        in_specs=in_specs,
        out_specs=pl.BlockSpec((tq, LANE), lambda b, h, i: (b * n_q + i, h)),
        scratch_shapes=[pltpu.VMEM((LANE, n_keys), BF16), pltpu.VMEM((n_keys, LANE), BF16)],
        compiler_params=_params("parallel", "parallel", "arbitrary"),
        name="diff_attention_ctx" if use_ctx else "diff_attention",
    )(*args)


def _conv_kernel(cb_ref, cc_ref, cx_ref, w_ref, o_ref, *, seq):
    u = cc_ref[...] * cx_ref[...]
    row = lax.broadcasted_iota(jnp.int32, u.shape, 0)
    prev = jnp.where(row == 0, 0.0, pltpu.roll(u, 1, 0))
    nxt = jnp.where(row == seq - 1, 0.0, pltpu.roll(u, seq - 1, 0))
    w = w_ref[...]
    conv = prev * w[0:1, :] + u * w[1:2, :] + nxt * w[2:3, :]
    o_ref[...] = (cb_ref[...] * conv).astype(o_ref.dtype)


def _gated_conv(proj, conv_w_t, layer, *, use_ctx):
    seq, nb, row0 = (DEC_SEQ, DEC_BATCH, T_PROMPT // DEC_SEQ) if use_ctx else (SEQ, BATCH, 0)
    n_c = BRANCH_W // LANE

    def col(c0):
        return pl.BlockSpec((seq, LANE), lambda b, c: (row0 + b, c0 + c))

    return pl.pallas_call(
        functools.partial(_conv_kernel, seq=seq),
        out_shape=jax.ShapeDtypeStruct((nb * seq, BRANCH_W), BF16),
        grid=(nb, n_c),
        in_specs=[col(COL_CB), col(COL_CC), col(COL_CX),
                  pl.BlockSpec((None, 3, LANE), lambda b, c: (layer, 0, c))],
        out_specs=pl.BlockSpec((seq, LANE), lambda b, c: (b, c)),
        compiler_params=_params("parallel", "parallel"),
        name="gated_conv_ctx" if use_ctx else "gated_conv",
    )(proj, proj, proj, conv_w_t)


def _dft_tables(n):
    j = jnp.arange(n, dtype=jnp.int32)
    m = (j[:, None] * j[None, :]) % n
    ang = m.astype(F32) * (2.0 * math.pi / n)
    return jnp.cos(ang), jnp.sin(ang)


def _fourier_kernel(x_ref, cw_ref, sw_ref, cs_ref, o_ref, y_ref, *, seq):
    @pl.when(pl.program_id(1) == 0)
    def _():
        for g in range(BRANCH_W // LANE):
            cols = slice(g * LANE, (g + 1) * LANE)
            xg = x_ref[:, cols].astype(BF16)
            y_ref[0:seq, cols] = jnp.dot(xg, cw_ref[...], preferred_element_type=F32).astype(BF16)
            y_ref[seq:2 * seq, cols] = (-jnp.dot(xg, sw_ref[...],
                                                 preferred_element_type=F32)).astype(BF16)

    o = jnp.dot(cs_ref[...], y_ref[...], preferred_element_type=F32)
    o_ref[...] = (o * ((seq * LANE) ** -0.5)).astype(o_ref.dtype)


def _fourier(proj, cw, sw, cs_l, *, use_ctx):
    seq, nb, row0 = (DEC_SEQ, DEC_BATCH, T_PROMPT // DEC_SEQ) if use_ctx else (SEQ, BATCH, 0)
    tl = 256
    n_t = seq // tl
    wide = BRANCH_W // LANE
    return pl.pallas_call(
        functools.partial(_fourier_kernel, seq=seq),
        out_shape=jax.ShapeDtypeStruct((nb * seq, BRANCH_W), BF16),
        grid=(nb, n_t),
        in_specs=[pl.BlockSpec((seq, BRANCH_W), lambda b, i: (row0 + b, COL_FX // wide)),
                  pl.BlockSpec((LANE, LANE), lambda b, i: (0, 0)),
                  pl.BlockSpec((LANE, LANE), lambda b, i: (0, 0)),
                  pl.BlockSpec((tl, 2 * seq), lambda b, i: (i, 0))],
        out_specs=pl.BlockSpec((tl, BRANCH_W), lambda b, i: (b * n_t + i, 0)),
        scratch_shapes=[pltpu.VMEM((2 * seq, BRANCH_W), BF16)],
        compiler_params=_params("parallel", "arbitrary"),
        name="fourier_ctx" if use_ctx else "fourier",
    )(proj, cw, sw, cs_l)


def _merge_kernel(h_ref, *refs):
    br_refs = refs[0:N_BRANCH]
    wg_refs = refs[N_BRANCH:2 * N_BRANCH]
    bg_refs = refs[2 * N_BRANCH:3 * N_BRANCH]
    wb_ref, o_ref, wgb_ref, wbb_ref = refs[3 * N_BRANCH:]

    @pl.when(pl.program_id(1) == 0)
    def _():
        for n in range(N_BRANCH):
            _cache_bf16(wg_refs[n], wgb_ref.at[n])
            wbb_ref[n] = wb_ref[n].astype(BF16)

    h = h_ref[...]
    acc = jnp.zeros(o_ref.shape, F32)
    for n in range(N_BRANCH):
        gate = jax.nn.sigmoid(jnp.dot(h, wgb_ref[n], preferred_element_type=F32) + bg_refs[n][...])
        acc = acc + gate * jnp.dot(br_refs[n][...], wbb_ref[n], preferred_element_type=F32)
    o_ref[...] = acc.astype(o_ref.dtype)


def _merge(h, branches, w_mgate, b_mgate, w_branch, layer):
    tm, tn = MM_TM, MERGE_TN
    n_t = D_MODEL // tn
    br_spec = pl.BlockSpec((tm, BRANCH_W), lambda j, i: (i, 0))

    def gate_w(n):
        return pl.BlockSpec((None, D_MODEL, tn), lambda j, i: (layer, 0, n * n_t + j))

    def gate_b(n):
        return pl.BlockSpec((None, 1, tn), lambda j, i: (layer, 0, n * n_t + j))

    return pl.pallas_call(
        _merge_kernel,
        out_shape=jax.ShapeDtypeStruct((T_ALL, D_MODEL), BF16),
        grid=(n_t, T_ALL // tm),
        in_specs=[pl.BlockSpec((tm, D_MODEL), lambda j, i: (i, 0))]
        + [br_spec] * N_BRANCH
        + [gate_w(n) for n in range(N_BRANCH)]
        + [gate_b(n) for n in range(N_BRANCH)]
        + [pl.BlockSpec((None, N_BRANCH, BRANCH_W, tn), lambda j, i: (layer, 0, 0, j))],
        out_specs=pl.BlockSpec((tm, tn), lambda j, i: (i, j)),
        scratch_shapes=[pltpu.VMEM((N_BRANCH, D_MODEL, tn), BF16),
                        pltpu.VMEM((N_BRANCH, BRANCH_W, tn), BF16)],
        compiler_params=_params("arbitrary", "arbitrary"),
        name="merge",
    )(h, *branches, *([w_mgate] * N_BRANCH),
      *([b_mgate.reshape(DEPTH, 1, N_BRANCH * D_MODEL)] * N_BRANCH), w_branch)


def _router_kernel(h_ref, w_ref, b_ref, idx_ref, wgt_ref):
    logits = jnp.dot(h_ref[...], w_ref[...].astype(BF16), preferred_element_type=F32) + b_ref[...]
    lane = lax.broadcasted_iota(jnp.int32, logits.shape, 1).astype(F32)
    neg = jnp.float32(-jnp.inf)
    logits = jnp.where(lane < N_EXPERTS, logits, neg)
    idx_out = jnp.zeros(logits.shape, F32)
    val_out = jnp.zeros(logits.shape, F32)
    top = None
    denom = None
    for k in range(TOP_K):
        m = jnp.max(logits, axis=-1, keepdims=True)
        idx = jnp.min(jnp.where(logits == m, lane, float(LANE)), axis=-1, keepdims=True)
        if k == 0:
            top = m
        e = jnp.exp(m - top)
        denom = e if k == 0 else denom + e
        idx_out = jnp.where(lane == k, idx, idx_out)
        val_out = jnp.where(lane == k, e, val_out)
        logits = jnp.where(lane == idx, neg, logits)
    idx_ref[...] = idx_out.astype(jnp.int32)
    wgt_ref[...] = val_out / denom


def _router(h, w_router_pad, b_router_pad, layer):
    row = pl.BlockSpec((ROW_TILE, LANE), lambda i: (i, 0))
    return pl.pallas_call(
        _router_kernel,
        out_shape=(jax.ShapeDtypeStruct((T_ALL, LANE), jnp.int32),
                   jax.ShapeDtypeStruct((T_ALL, LANE), F32)),
        grid=(T_ALL // ROW_TILE,),
        in_specs=[pl.BlockSpec((ROW_TILE, D_MODEL), lambda i: (i, 0)),
                  pl.BlockSpec((None, D_MODEL, LANE), lambda i: (layer, 0, 0)),
                  pl.BlockSpec((None, 1, LANE), lambda i: (layer, 0, 0))],
        out_specs=(row, row),
        compiler_params=_params("parallel"),
        name="router",
    )(h, w_router_pad, b_router_pad)


def _expert_changed(tile_e_ref):
    m = pl.program_id(1)
    prev = tile_e_ref[jnp.maximum(m - 1, 0)]
    return jnp.logical_or(m == 0, tile_e_ref[m] != prev)


def _expert_up_kernel(tile_e_ref, nvalid_ref, x_ref, wg_ref, wl_ref, bg_ref, bl_ref, a_ref,
                      wgb_ref, wlb_ref):
    @pl.when(_expert_changed(tile_e_ref))
    def _():
        _cache_bf16(wg_ref, wgb_ref)
        _cache_bf16(wl_ref, wlb_ref)

    @pl.when(pl.program_id(1) < nvalid_ref[0])
    def _():
        x = x_ref[...]
        gate = jnp.dot(x, wgb_ref[...], preferred_element_type=F32) + bg_ref[...]
        lin = jnp.dot(x, wlb_ref[...], preferred_element_type=F32) + bl_ref[...]
        gate = jnp.minimum(gate, SWIGLU_LIMIT)
        lin = jnp.clip(lin, -SWIGLU_LIMIT, SWIGLU_LIMIT)
        a_ref[...] = ((lin + 1.0) * gate * jax.nn.sigmoid(SWIGLU_ALPHA * gate)).astype(a_ref.dtype)

    @pl.when(pl.program_id(1) >= nvalid_ref[0])
    def _():
        a_ref[...] = jnp.zeros(a_ref.shape, a_ref.dtype)


def _expert_down_kernel(tile_e_ref, nvalid_ref, a_ref, w_ref, b_ref, y_ref, wb_ref):
    @pl.when(_expert_changed(tile_e_ref))
    def _():
        _cache_bf16(w_ref, wb_ref)

    @pl.when(pl.program_id(1) < nvalid_ref[0])
    def _():
        y_ref[...] = jnp.dot(a_ref[...], wb_ref[...], preferred_element_type=F32) + b_ref[...]

    @pl.when(pl.program_id(1) >= nvalid_ref[0])
    def _():
        y_ref[...] = jnp.zeros(y_ref.shape, y_ref.dtype)


def _experts(xs, tile_e, nvalid, w_gu, b_gu, w_down, b_down, layer):
    n_f = D_EXPERT // MOE_TF
    act = pl.pallas_call(
        _expert_up_kernel,
        out_shape=jax.ShapeDtypeStruct((P_PAD, D_EXPERT), BF16),
        grid_spec=pltpu.PrefetchScalarGridSpec(
            num_scalar_prefetch=2,
            grid=(n_f, N_MOE_TILES),
            in_specs=[
                pl.BlockSpec((MOE_TM, D_MODEL), lambda j, m, te, nv: (m, 0)),
                pl.BlockSpec((None, None, D_MODEL, MOE_TF), lambda j, m, te, nv: (layer, te[m], 0, j)),
                pl.BlockSpec((None, None, D_MODEL, MOE_TF),
                             lambda j, m, te, nv: (layer, te[m], 0, n_f + j)),
                pl.BlockSpec((None, None, 1, MOE_TF), lambda j, m, te, nv: (layer, te[m], 0, j)),
                pl.BlockSpec((None, None, 1, MOE_TF),
                             lambda j, m, te, nv: (layer, te[m], 0, n_f + j)),
            ],
            out_specs=pl.BlockSpec((MOE_TM, MOE_TF), lambda j, m, te, nv: (m, j)),
            scratch_shapes=[pltpu.VMEM((D_MODEL, MOE_TF), BF16), pltpu.VMEM((D_MODEL, MOE_TF), BF16)],
        ),
        compiler_params=_params("arbitrary", "arbitrary"),
        name="expert_up",
    )(tile_e, nvalid, xs, w_gu, w_gu, b_gu.reshape(DEPTH, N_EXPERTS, 1, 2 * D_EXPERT),
      b_gu.reshape(DEPTH, N_EXPERTS, 1, 2 * D_EXPERT))
    return pl.pallas_call(
        _expert_down_kernel,
        out_shape=jax.ShapeDtypeStruct((P_PAD, D_MODEL), F32),
        grid_spec=pltpu.PrefetchScalarGridSpec(
            num_scalar_prefetch=2,
            grid=(D_MODEL // MOE_TN, N_MOE_TILES),
            in_specs=[
                pl.BlockSpec((MOE_TM, D_EXPERT), lambda j, m, te, nv: (m, 0)),
                pl.BlockSpec((None, None, D_EXPERT, MOE_TN), lambda j, m, te, nv: (layer, te[m], 0, j)),
                pl.BlockSpec((None, None, 1, MOE_TN), lambda j, m, te, nv: (layer, te[m], 0, j)),
            ],
            out_specs=pl.BlockSpec((MOE_TM, MOE_TN), lambda j, m, te, nv: (m, j)),
            scratch_shapes=[pltpu.VMEM((D_EXPERT, MOE_TN), BF16)],
        ),
        compiler_params=_params("arbitrary", "arbitrary"),
        name="expert_down",
    )(tile_e, nvalid, act, w_down, b_down.reshape(DEPTH, N_EXPERTS, 1, D_MODEL))


def _dispatch_plan(top_i):
    flat_e = top_i.reshape(-1)
    onehot = (flat_e[:, None] == jnp.arange(N_EXPERTS, dtype=jnp.int32)[None, :]).astype(jnp.int32)
    cum = jnp.cumsum(onehot, axis=0)
    rank = jnp.take_along_axis(cum, flat_e[:, None], axis=1)[:, 0] - 1
    counts = cum[-1]
    padded = ((counts + MOE_TM - 1) // MOE_TM) * MOE_TM
    gend = jnp.cumsum(padded)
    dest = (gend - padded)[flat_e] + rank
    src_tok = jnp.zeros((P_PAD,), jnp.int32).at[dest].set(
        jnp.arange(P_PAIRS, dtype=jnp.int32) // TOP_K)
    tile_start = jnp.arange(N_MOE_TILES, dtype=jnp.int32) * MOE_TM
    tile_e = jnp.minimum(jnp.searchsorted(gend, tile_start, side="right"),
                         N_EXPERTS - 1).astype(jnp.int32)
    nvalid = (gend[-1:] // MOE_TM).astype(jnp.int32)
    return dest, src_tok, tile_e, nvalid


def _moe(h, w_router_pad, b_router_pad, w_gu, b_gu, w_down, b_down, layer):
    idx, wgt = _router(h, w_router_pad, b_router_pad, layer)
    top_i, top_w = idx[:, :TOP_K], wgt[:, :TOP_K]
    dest, src_tok, tile_e, nvalid = _dispatch_plan(top_i)
    xs = jnp.take(h, src_tok, axis=0)
    y = _experts(xs, tile_e, nvalid, w_gu, b_gu, w_down, b_down, layer)
    picked = jnp.take(y, dest, axis=0).reshape(T_ALL, TOP_K, D_MODEL)
    return jnp.sum(picked * top_w[:, :, None], axis=1)


def kernel(x_prompt, x_sample, c, cache_k, cache_v, state_ret, c_ctx, w_mod, b_mod, w_in,
           ret_decay_logit, diff_lambda, conv_w, w_branch, w_mgate, b_mgate, w_out,
           ln1_g, ln1_b, ln2_g, ln2_b, w_router, b_router, w_gu, b_gu, w_down, b_down):
    x = jnp.concatenate([x_prompt.reshape(T_PROMPT, D_MODEL), x_sample.reshape(T_SAMPLE, D_MODEL)], 0)
    cond = jnp.concatenate([c_ctx[None, :], c, jnp.zeros((COND_PAD - N_COND, D_MODEL), F32)], 0)
    ret_tables = _rope_tables(RET_DK)
    diff_tables = _rope_tables(DIFF_DH)
    cw, sw = (t.astype(BF16) for t in _dft_tables(LANE))
    cs_p = jnp.concatenate(_dft_tables(SEQ), axis=1).astype(BF16)
    cs_s = jnp.concatenate(_dft_tables(DEC_SEQ), axis=1).astype(BF16)
    conv_w_t = jnp.swapaxes(conv_w, 1, 2)
    w_router_pad = jnp.pad(w_router, ((0, 0), (0, 0), (0, LANE - N_EXPERTS)))
    b_router_pad = jnp.pad(b_router, ((0, 0), (0, LANE - N_EXPERTS))).reshape(DEPTH, 1, LANE)

    mods = [_modulation(cond, w_mod, b_mod, l).reshape(COND_PAD, 6, 1, D_MODEL)
            for l in range(DEPTH)]
    ks_out, vs_out, ss_out = [], [], []
    h = _ln_mod(x, mods[0], 1, 0)
    for l in range(DEPTH):
        lam_init = 0.8 - 0.6 * math.exp(-0.3 * l)
        proj = _matmul(h, w_in, l, F32, "in_proj")
        proj_p = proj[:T_PROMPT]
        ks_out.append(proj_p[:, COL_DK * LANE:COL_DV * LANE]
                      .reshape(BATCH, SEQ, DIFF_HEADS, 2, DIFF_DH))
        vs_out.append(proj_p[:, COL_DV * LANE:COL_CB * LANE]
                      .reshape(BATCH, SEQ, DIFF_HEADS, 2 * DIFF_DH))
        ret_p, s_fin = _retention(proj, ret_decay_logit, l, use_ctx=False)
        ret_s = _retention(proj, ret_decay_logit, l, use_ctx=True, state_ret=state_ret,
                           tables=ret_tables)
        ss_out.append(s_fin)
        diff_p = _diff_attention(proj, diff_lambda, l, lam_init, use_ctx=False)
        diff_s = _diff_attention(proj, diff_lambda, l, lam_init, use_ctx=True, cache_k=cache_k,
                                 cache_v=cache_v, tables=diff_tables)
        conv_p = _gated_conv(proj, conv_w_t, l, use_ctx=False)
        conv_s = _gated_conv(proj, conv_w_t, l, use_ctx=True)
        four_p = _fourier(proj, cw, sw, cs_p, use_ctx=False)
        four_s = _fourier(proj, cw, sw, cs_s, use_ctx=True)
        branches = [jnp.concatenate(pair, axis=0) for pair in
                    ((ret_p, ret_s), (diff_p, diff_s), (conv_p, conv_s), (four_p, four_s))]
        merged = _merge(h, branches, w_mgate, b_mgate, w_branch, l)
        mix = _matmul(merged, w_out, l, F32, "out_proj")
        x, h2 = _resid_ln(x, mix, mods[l], 2, ln1_g, ln1_b, l, mods[l], 4, 3)
        moe = _moe(h2, w_router_pad, b_router_pad, w_gu, b_gu, w_down, b_down, l)
        nxt = mods[min(l + 1, DEPTH - 1)]
        x, h = _resid_ln(x, moe, mods[l], 5, ln2_g, ln2_b, l, nxt, 1, 0)
    y_p = x[:T_PROMPT].reshape(BATCH, SEQ, D_MODEL)
    y_s = x[T_PROMPT:].reshape(DEC_BATCH, DEC_SEQ, D_MODEL)
    return (y_p, y_s, jnp.stack(ks_out, axis=1), jnp.stack(vs_out, axis=1),
            jnp.stack(ss_out, axis=1))
```

```python
import functools
import math

import jax
import jax.numpy as jnp
import numpy as np
from jax import lax
from jax.experimental import pallas as pl
from jax.experimental.pallas import tpu as pltpu

D_MODEL = 2048
BATCH = 16
SEQ = 256
DEPTH = 4
DEC_BATCH = 4
DEC_SEQ = 2048
PAST_LEN = 512
GRID_W = 64
N_BRANCH = 4
BRANCH_W = 512
RET_HEADS = 4
RET_DK = 128
RET_CHUNK = 128
DIFF_HEADS = 4
DIFF_DH = 64
N_EXPERTS = 32
TOP_K = 4
D_EXPERT = 2048
SWIGLU_ALPHA = 1.702
SWIGLU_LIMIT = 7.0
ROPE_BASE = 10000.0
LN_EPS = 1e-5
RMS_EPS = 1e-6
DEEP_ALPHA = (2 * DEPTH) ** 0.25
D_IN = 5632

LANE = 128
T_PROMPT = BATCH * SEQ
T_SAMPLE = DEC_BATCH * DEC_SEQ
T_ALL = T_PROMPT + T_SAMPLE
N_COND = 1 + DEC_BATCH
COND_PAD = 8

ROW_TILE = 256
MM_TM = 512
MM_TN = 512
IN_TN = 1408
MERGE_TN = 256
MOE_TM = 256
MOE_TF = 1024
MOE_TN = 1024
CAST_ROWS = 256
P_PAIRS = T_ALL * TOP_K
P_PAD = P_PAIRS + N_EXPERTS * MOE_TM
N_MOE_TILES = P_PAD // MOE_TM
VMEM_LIMIT = 56 * 1024 * 1024

COL_RQ, COL_RK, COL_RV, COL_RG = 0, 4, 8, 12
COL_DQ, COL_DK, COL_DV = 16, 20, 24
COL_CB, COL_CC, COL_CX = 28, 32, 36
COL_FX = 40

BF16 = jnp.bfloat16
F32 = jnp.float32


def _params(*sem):
    return pltpu.CompilerParams(dimension_semantics=sem, vmem_limit_bytes=VMEM_LIMIT)


def _cond_row_of_tile(i, tile):
    start = i * tile
    return jnp.where(start < T_PROMPT, 0, 1 + (start - T_PROMPT) // DEC_SEQ)


def _cache_bf16(src_ref, dst_ref):
    n = src_ref.shape[0] // CAST_ROWS

    def body(i, carry):
        r = pl.multiple_of(i * CAST_ROWS, CAST_ROWS)
        dst_ref[pl.ds(r, CAST_ROWS), :] = src_ref[pl.ds(r, CAST_ROWS), :].astype(BF16)
        return carry

    lax.fori_loop(0, n, body, 0)


def _mod_kernel(c_ref, w_ref, b_ref, o_ref):
    c = c_ref[...]
    a = (c * jax.nn.sigmoid(c)).astype(BF16)
    o_ref[...] = jnp.dot(a, w_ref[...].astype(BF16), preferred_element_type=F32) + b_ref[...]


def _modulation(cond_pad, w_mod, b_mod, layer):
    n = 6 * D_MODEL
    return pl.pallas_call(
        _mod_kernel,
        out_shape=jax.ShapeDtypeStruct((COND_PAD, n), F32),
        grid=(n // MM_TN,),
        in_specs=[
            pl.BlockSpec((COND_PAD, D_MODEL), lambda j: (0, 0)),
            pl.BlockSpec((None, D_MODEL, MM_TN), lambda j: (layer, 0, j)),
            pl.BlockSpec((None, 1, MM_TN), lambda j: (layer, 0, j)),
        ],
        out_specs=pl.BlockSpec((COND_PAD, MM_TN), lambda j: (0, j)),
        compiler_params=_params("arbitrary"),
        name="modulation",
    )(cond_pad, w_mod, b_mod.reshape(DEPTH, 1, n))


def _ln(x):
    xc = x - jnp.mean(x, axis=-1, keepdims=True)
    return xc * lax.rsqrt(jnp.mean(xc * xc, axis=-1, keepdims=True) + LN_EPS)


def _ln_mod_kernel(x_ref, sc_ref, sh_ref, h_ref):
    h_ref[...] = (_ln(x_ref[...]) * (1.0 + sc_ref[...]) + sh_ref[...]).astype(h_ref.dtype)


def _mod_spec(which):
    return pl.BlockSpec((None, None, 1, D_MODEL),
                        lambda i, *_: (_cond_row_of_tile(i, ROW_TILE), which, 0, 0))


def _ln_mod(x, mod4, sc_idx, sh_idx):
    return pl.pallas_call(
        _ln_mod_kernel,
        out_shape=jax.ShapeDtypeStruct((T_ALL, D_MODEL), BF16),
        grid=(T_ALL // ROW_TILE,),
        in_specs=[pl.BlockSpec((ROW_TILE, D_MODEL), lambda i: (i, 0)),
                  _mod_spec(sc_idx), _mod_spec(sh_idx)],
        out_specs=pl.BlockSpec((ROW_TILE, D_MODEL), lambda i: (i, 0)),
        compiler_params=_params("parallel"),
        name="ln_mod",
    )(x, mod4, mod4)


def _resid_ln_kernel(x_ref, y_ref, g_ref, lg_ref, lb_ref, sc_ref, sh_ref, xo_ref, h_ref):
    z = DEEP_ALPHA * x_ref[...] + g_ref[...] * y_ref[...]
    xn = _ln(z) * lg_ref[...] + lb_ref[...]
    xo_ref[...] = xn
    h_ref[...] = (_ln(xn) * (1.0 + sc_ref[...]) + sh_ref[...]).astype(h_ref.dtype)


def _resid_ln(x, y, mod_gate, gate_idx, ln_g, ln_b, layer, mod_next, sc_idx, sh_idx, h_dtype):
    row = pl.BlockSpec((ROW_TILE, D_MODEL), lambda i: (i, 0))
    vec = pl.BlockSpec((None, 1, D_MODEL), lambda i: (layer, 0, 0))
    return pl.pallas_call(
        _resid_ln_kernel,
        out_shape=(jax.ShapeDtypeStruct((T_ALL, D_MODEL), F32),
                   jax.ShapeDtypeStruct((T_ALL, D_MODEL), h_dtype)),
        grid=(T_ALL // ROW_TILE,),
        in_specs=[row, row, _mod_spec(gate_idx), vec, vec, _mod_spec(sc_idx), _mod_spec(sh_idx)],
        out_specs=(row, row),
        compiler_params=_params("parallel"),
        name="resid_ln",
    )(x, y, mod_gate, ln_g.reshape(DEPTH, 1, D_MODEL), ln_b.reshape(DEPTH, 1, D_MODEL),
      mod_next, mod_next)


def _mm_kernel(x_ref, w_ref, o_ref, wb_ref):
    @pl.when(pl.program_id(1) == 0)
    def _():
        _cache_bf16(w_ref, wb_ref)

    o_ref[...] = jnp.dot(x_ref[...], wb_ref[...], preferred_element_type=F32).astype(o_ref.dtype)


def _matmul(x, w, layer, out_dtype, tn, name):
    m, k = x.shape
    n = w.shape[-1]
    return pl.pallas_call(
        _mm_kernel,
        out_shape=jax.ShapeDtypeStruct((m, n), out_dtype),
        grid=(n // tn, m // MM_TM),
        in_specs=[pl.BlockSpec((MM_TM, k), lambda j, i: (i, 0)),
                  pl.BlockSpec((None, k, tn), lambda j, i: (layer, 0, j))],
        out_specs=pl.BlockSpec((MM_TM, tn), lambda j, i: (i, j)),
        scratch_shapes=[pltpu.VMEM((k, tn), BF16)],
        compiler_params=_params("arbitrary", "arbitrary"),
        name=name,
    )(x, w)


def _rope_tables(d):
    rows = DEC_SEQ // GRID_W
    row = jnp.repeat(jnp.arange(rows, dtype=F32), GRID_W)
    col = jnp.tile(jnp.arange(GRID_W, dtype=F32), rows)
    axis_dim = d // 2
    inv_freq = ROPE_BASE ** (-jnp.arange(0, axis_dim, 2, dtype=F32) / axis_dim)
    ang_r = row[:, None] * inv_freq
    ang_c = col[:, None] * inv_freq
    cos = jnp.concatenate([jnp.cos(ang_r), jnp.cos(ang_r), jnp.cos(ang_c), jnp.cos(ang_c)], -1)
    sin = jnp.concatenate([-jnp.sin(ang_r), jnp.sin(ang_r), -jnp.sin(ang_c), jnp.sin(ang_c)], -1)
    reps = LANE // d
    return jnp.tile(cos, (1, reps)), jnp.tile(sin, (1, reps))


def _rope(x, cos, sin, d):
    q = d // 4
    lane = lax.broadcasted_iota(jnp.int32, x.shape, 1)
    first = (lane % (2 * q)) < q
    partner = jnp.where(first, pltpu.roll(x, LANE - q, 1), pltpu.roll(x, q, 1))
    return x * cos + partner * sin


def _retention_kernel(*refs, seq, use_ctx):
    if use_ctx:
        (q_ref, k_ref, v_ref, g_ref, lgt_ref, s0_ref, cos_ref, sin_ref, o_ref,
         of_ref, ob_ref) = refs
    else:
        q_ref, k_ref, v_ref, g_ref, lgt_ref, o_ref, sfin_ref, of_ref, ob_ref = refs
    c = RET_CHUNK
    n_chunk = seq // c
    ii = lax.broadcasted_iota(jnp.int32, (c, c), 0).astype(F32)
    jj = lax.broadcasted_iota(jnp.int32, (c, c), 1).astype(F32)
    col_pos = lax.broadcasted_iota(jnp.int32, (c, 1), 0).astype(F32)
    row_pos = lax.broadcasted_iota(jnp.int32, (1, c), 1).astype(F32)

    def direction(d):
        logit = lgt_ref[d]
        lg = jnp.minimum(logit, 0.0) - jnp.log1p(jnp.exp(-jnp.abs(logit)))
        lg_mat = jnp.broadcast_to(lg[0:1, :], (c, c))
        lg_col = lg[:, 0:1][0:1, :]
        if d == 0:
            dist = ii - jj
            kw = jnp.exp((c - 1 - row_pos) * lg_col)
            qw = jnp.exp((col_pos + 1.0) * lg_col)
        else:
            dist = jj - ii
            kw = jnp.exp(row_pos * lg_col)
            qw = jnp.exp((c - col_pos) * lg_col)
        decay = jnp.where(dist >= 0, jnp.exp(jnp.where(dist >= 0, dist, 0.0) * lg_mat), 0.0)
        chunk_decay = jnp.exp(c * lg_mat)
        return decay, kw, qw, chunk_decay

    def chunk(n):
        rows = slice(n * c, (n + 1) * c)
        q = q_ref[rows, :]
        k = k_ref[rows, :] * (RET_DK ** -0.5)
        if use_ctx:
            q = _rope(q, cos_ref[rows, :], sin_ref[rows, :], RET_DK)
            k = _rope(k, cos_ref[rows, :], sin_ref[rows, :], RET_DK)
        return q, k.T, v_ref[rows, :].astype(BF16)

    for d, acc_ref in ((0, of_ref), (1, ob_ref)):
        decay, kw, qw, chunk_decay = direction(d)
        if use_ctx:
            state = s0_ref[d]
        else:
            state = jnp.zeros((RET_DK, RET_DK), F32)
        order = range(n_chunk) if d == 0 else range(n_chunk - 1, -1, -1)
        for n in order:
            q, kt, v = chunk(n)
            scores = jnp.dot(q.astype(BF16), kt.astype(BF16), preferred_element_type=F32) * decay
            o = jnp.dot(scores.astype(BF16), v, preferred_element_type=F32)
            o = o + jnp.dot((q * qw).astype(BF16), state.astype(BF16), preferred_element_type=F32)
            acc_ref[n * c:(n + 1) * c, :] = o
            state = chunk_decay * state + jnp.dot((kt * kw).astype(BF16), v,
                                                  preferred_element_type=F32)
        if not use_ctx:
            sfin_ref[d] = state

    o = of_ref[...] + ob_ref[...]
    o = o * lax.rsqrt(jnp.mean(o * o, axis=-1, keepdims=True) + RMS_EPS)
    g = g_ref[...]
    o_ref[...] = (g * jax.nn.sigmoid(g) * o).astype(o_ref.dtype)


def _retention(proj, decay_logit, layer, *, use_ctx, state_ret=None, tables=None):
    seq, nb, row0 = (DEC_SEQ, DEC_BATCH, T_PROMPT // DEC_SEQ) if use_ctx else (SEQ, BATCH, 0)

    def col(c0):
        return pl.BlockSpec((seq, LANE), lambda b, h: (row0 + b, c0 + h))

    lgt = jnp.broadcast_to(decay_logit[layer][:, :, None, None], (2, RET_HEADS, 8, LANE))
    in_specs = [col(COL_RQ), col(COL_RK), col(COL_RV), col(COL_RG),
                pl.BlockSpec((2, None, 8, LANE), lambda b, h: (0, h, 0, 0))]
    args = [proj, proj, proj, proj, lgt]
    out_branch = jax.ShapeDtypeStruct((nb * seq, BRANCH_W), BF16)
    branch_spec = pl.BlockSpec((seq, LANE), lambda b, h: (b, h))
    if use_ctx:
        in_specs += [pl.BlockSpec((None, None, 2, None, RET_DK, RET_DK),
                                  lambda b, h: (b, layer, 0, h, 0, 0)),
                     pl.BlockSpec((seq, LANE), lambda b, h: (0, 0)),
                     pl.BlockSpec((seq, LANE), lambda b, h: (0, 0))]
        args += [state_ret, tables[0], tables[1]]
        out_shape, out_specs = out_branch, branch_spec
    else:
        out_shape = (out_branch,
                     jax.ShapeDtypeStruct((nb, 2, RET_HEADS, RET_DK, RET_DK), F32))
        out_specs = (branch_spec,
                     pl.BlockSpec((None, 2, None, RET_DK, RET_DK), lambda b, h: (b, 0, h, 0, 0)))
    return pl.pallas_call(
        functools.partial(_retention_kernel, seq=seq, use_ctx=use_ctx),
        out_shape=out_shape,
        grid=(nb, RET_HEADS),
        in_specs=in_specs,
        out_specs=out_specs,
        scratch_shapes=[pltpu.VMEM((seq, LANE), F32), pltpu.VMEM((seq, LANE), F32)],
        compiler_params=_params("parallel", "parallel"),
        name="retention_ctx" if use_ctx else "retention",
    )(*args)


def _diff_attn_kernel(*refs, seq, tq, use_ctx, lam_init):
    if use_ctx:
        (q_ref, k_ref, v_ref, lam_ref, kc_ref, vc_ref, cosq_ref, sinq_ref, cosk_ref, sink_ref,
         o_ref, kt_ref, va_ref) = refs
    else:
        q_ref, k_ref, v_ref, lam_ref, o_ref, kt_ref, va_ref = refs
    c = LANE

    @pl.when(pl.program_id(2) == 0)
    def _():
        for n in range(seq // c):
            rows = slice(n * c, (n + 1) * c)
            k = k_ref[rows, :]
            if use_ctx:
                k = _rope(k, cosk_ref[rows, :], sink_ref[rows, :], DIFF_DH)
            kt_ref[:, rows] = k.T.astype(BF16)
        va_ref[0:seq, :] = v_ref[...].astype(BF16)
        if use_ctx:
            for n in range(PAST_LEN // c):
                rows = slice(n * c, (n + 1) * c)
                kt_ref[:, seq + n * c:seq + (n + 1) * c] = kc_ref[rows, :].T.astype(BF16)
            va_ref[seq:seq + PAST_LEN, :] = vc_ref[...].astype(BF16)

    lam_v = lam_ref[...]
    lam = (jnp.exp(jnp.sum(lam_v[0:1] * lam_v[1:2], axis=-1, keepdims=True))
           - jnp.exp(jnp.sum(lam_v[2:3] * lam_v[3:4], axis=-1, keepdims=True)) + lam_init)
    q = q_ref[...]
    if use_ctx:
        q = _rope(q, cosq_ref[...], sinq_ref[...], DIFF_DH)
    lane = lax.broadcasted_iota(jnp.int32, q.shape, 1)
    scale = DIFF_DH ** -0.5
    kt = kt_ref[...]
    probs = []
    for m in range(2):
        qm = jnp.where((lane < DIFF_DH) == (m == 0), q, 0.0).astype(BF16)
        s = jnp.dot(qm, kt, preferred_element_type=F32) * scale
        p = jnp.exp(s - jnp.max(s, axis=-1, keepdims=True))
        probs.append(p / jnp.sum(p, axis=-1, keepdims=True))
    w = probs[0] - lam * probs[1]
    o = jnp.dot(w.astype(BF16), va_ref[...], preferred_element_type=F32)
    o = o * lax.rsqrt(jnp.mean(o * o, axis=-1, keepdims=True) + RMS_EPS) * (1.0 - lam_init)
    o_ref[...] = o.astype(o_ref.dtype)


def _diff_attention(proj, diff_lambda, layer, lam_init, *, use_ctx, cache_k=None, cache_v=None,
                    tables=None):
    seq, nb, row0 = (DEC_SEQ, DEC_BATCH, T_PROMPT // DEC_SEQ) if use_ctx else (SEQ, BATCH, 0)
    tq = 256
    n_q = seq // tq
    n_keys = seq + (PAST_LEN if use_ctx else 0)
    in_specs = [pl.BlockSpec((tq, LANE), lambda b, h, i: ((row0 + b) * n_q + i, COL_DQ + h)),
                pl.BlockSpec((seq, LANE), lambda b, h, i: (row0 + b, COL_DK + h)),
                pl.BlockSpec((seq, LANE), lambda b, h, i: (row0 + b, COL_DV + h)),
                pl.BlockSpec((None, 4, DIFF_DH), lambda b, h, i: (layer, 0, 0))]
    args = [proj, proj, proj, diff_lambda]
    if use_ctx:
        ctx_spec = pl.BlockSpec((None, None, PAST_LEN, LANE), lambda b, h, i: (b, layer, 0, h))
        full = pl.BlockSpec((seq, LANE), lambda b, h, i: (0, 0))
        qtab = pl.BlockSpec((tq, LANE), lambda b, h, i: (i, 0))
        in_specs += [ctx_spec, ctx_spec, qtab, qtab, full, full]
        args += [cache_k.reshape(DEC_BATCH, DEPTH, PAST_LEN, DIFF_HEADS * 2 * DIFF_DH),
                 cache_v.reshape(DEC_BATCH, DEPTH, PAST_LEN, DIFF_HEADS * 2 * DIFF_DH),
                 tables[0], tables[1], tables[0], tables[1]]
    return pl.pallas_call(
        functools.partial(_diff_attn_kernel, seq=seq, tq=tq, use_ctx=use_ctx, lam_init=lam_init),
        out_shape=jax.ShapeDtypeStruct((nb * seq, BRANCH_W), BF16),
        grid=(nb, DIFF_HEADS, n_q),
        in_specs=in_specs,
        out_specs=pl.BlockSpec((tq, LANE), lambda b, h, i: (b * n_q + i, h)),
        scratch_shapes=[pltpu.VMEM((LANE, n_keys), BF16), pltpu.VMEM((n_keys, LANE), BF16)],
        compiler_params=_params("parallel", "parallel", "arbitrary"),
        name="diff_attention_ctx" if use_ctx else "diff_attention",
    )(*args)


def _conv_kernel(cb_ref, cc_ref, cx_ref, w_ref, o_ref, *, seq):
    u = cc_ref[...] * cx_ref[...]
    row = lax.broadcasted_iota(jnp.int32, u.shape, 0)
    prev = jnp.where(row == 0, 0.0, pltpu.roll(u, 1, 0))
    nxt = jnp.where(row == seq - 1, 0.0, pltpu.roll(u, seq - 1, 0))
    w = w_ref[...]
    conv = prev * w[0:1, :] + u * w[1:2, :] + nxt * w[2:3, :]
    o_ref[...] = (cb_ref[...] * conv).astype(o_ref.dtype)


def _gated_conv(proj, conv_w_t, layer, *, use_ctx):
    seq, nb, row0 = (DEC_SEQ, DEC_BATCH, T_PROMPT // DEC_SEQ) if use_ctx else (SEQ, BATCH, 0)
    n_c = BRANCH_W // LANE

    def col(c0):
        return pl.BlockSpec((seq, LANE), lambda b, c: (row0 + b, c0 + c))

    return pl.pallas_call(
        functools.partial(_conv_kernel, seq=seq),
        out_shape=jax.ShapeDtypeStruct((nb * seq, BRANCH_W), BF16),
        grid=(nb, n_c),
        in_specs=[col(COL_CB), col(COL_CC), col(COL_CX),
                  pl.BlockSpec((None, 3, LANE), lambda b, c: (layer, 0, c))],
        out_specs=pl.BlockSpec((seq, LANE), lambda b, c: (b, c)),
        compiler_params=_params("parallel", "parallel"),
        name="gated_conv_ctx" if use_ctx else "gated_conv",
    )(proj, proj, proj, conv_w_t)


def _dft_tables(n):
    j = jnp.arange(n, dtype=jnp.int32)
    m = (j[:, None] * j[None, :]) % n
    ang = m.astype(F32) * (2.0 * math.pi / n)
    return jnp.cos(ang), jnp.sin(ang)


def _fourier_kernel(x_ref, cw_ref, sw_ref, cs_ref, o_ref, y_ref, *, seq):
    @pl.when(pl.program_id(1) == 0)
    def _():
        for g in range(BRANCH_W // LANE):
            cols = slice(g * LANE, (g + 1) * LANE)
            xg = x_ref[:, cols].astype(BF16)
            y_ref[0:seq, cols] = jnp.dot(xg, cw_ref[...], preferred_element_type=F32).astype(BF16)
            y_ref[seq:2 * seq, cols] = (-jnp.dot(xg, sw_ref[...],
                                                 preferred_element_type=F32)).astype(BF16)

    o = jnp.dot(cs_ref[...], y_ref[...], preferred_element_type=F32)
    o_ref[...] = (o * ((seq * LANE) ** -0.5)).astype(o_ref.dtype)


def _fourier(proj, cw, sw, cs_l, *, use_ctx):
    seq, nb, row0 = (DEC_SEQ, DEC_BATCH, T_PROMPT // DEC_SEQ) if use_ctx else (SEQ, BATCH, 0)
    tl = 256
    n_t = seq // tl
    wide = BRANCH_W // LANE
    return pl.pallas_call(
        functools.partial(_fourier_kernel, seq=seq),
        out_shape=jax.ShapeDtypeStruct((nb * seq, BRANCH_W), BF16),
        grid=(nb, n_t),
        in_specs=[pl.BlockSpec((seq, BRANCH_W), lambda b, i: (row0 + b, COL_FX // wide)),
                  pl.BlockSpec((LANE, LANE), lambda b, i: (0, 0)),
                  pl.BlockSpec((LANE, LANE), lambda b, i: (0, 0)),
                  pl.BlockSpec((tl, 2 * seq), lambda b, i: (i, 0))],
        out_specs=pl.BlockSpec((tl, BRANCH_W), lambda b, i: (b * n_t + i, 0)),
        scratch_shapes=[pltpu.VMEM((2 * seq, BRANCH_W), BF16)],
        compiler_params=_params("parallel", "arbitrary"),
        name="fourier_ctx" if use_ctx else "fourier",
    )(proj, cw, sw, cs_l)


def _merge_kernel(h_ref, *refs):
    ctx_refs = refs[0:N_BRANCH]
    lat_refs = refs[N_BRANCH:2 * N_BRANCH]
    wg_refs = refs[2 * N_BRANCH:3 * N_BRANCH]
    bg_refs = refs[3 * N_BRANCH:4 * N_BRANCH]
    wb_ref, o_ref, wgb_ref, wbb_ref = refs[4 * N_BRANCH:]
    is_ctx = pl.program_id(1) < T_PROMPT // MM_TM

    @pl.when(pl.program_id(1) == 0)
    def _():
        for n in range(N_BRANCH):
            _cache_bf16(wg_refs[n], wgb_ref.at[n])
            wbb_ref[n] = wb_ref[n].astype(BF16)

    h = h_ref[...]
    acc = jnp.zeros(o_ref.shape, F32)
    for n in range(N_BRANCH):
        gate = jax.nn.sigmoid(jnp.dot(h, wgb_ref[n], preferred_element_type=F32) + bg_refs[n][...])
        branch = jnp.where(is_ctx, ctx_refs[n][...], lat_refs[n][...])
        acc = acc + gate * jnp.dot(branch, wbb_ref[n], preferred_element_type=F32)
    o_ref[...] = acc.astype(o_ref.dtype)


def _merge(h, ctx_branches, lat_branches, w_mgate, b_mgate, w_branch, layer):
    tm, tn = MM_TM, MERGE_TN
    n_t = D_MODEL // tn
    n_ctx = T_PROMPT // tm
    ctx_spec = pl.BlockSpec((tm, BRANCH_W), lambda j, i: (jnp.minimum(i, n_ctx - 1), 0))
    lat_spec = pl.BlockSpec((tm, BRANCH_W), lambda j, i: (jnp.maximum(i - n_ctx, 0), 0))

    def gate_w(n):
        return pl.BlockSpec((None, D_MODEL, tn), lambda j, i: (layer, 0, n * n_t + j))

    def gate_b(n):
        return pl.BlockSpec((None, 1, tn), lambda j, i: (layer, 0, n * n_t + j))

    return pl.pallas_call(
        _merge_kernel,
        out_shape=jax.ShapeDtypeStruct((T_ALL, D_MODEL), BF16),
        grid=(n_t, T_ALL // tm),
        in_specs=[pl.BlockSpec((tm, D_MODEL), lambda j, i: (i, 0))]
        + [ctx_spec] * N_BRANCH + [lat_spec] * N_BRANCH
        + [gate_w(n) for n in range(N_BRANCH)]
        + [gate_b(n) for n in range(N_BRANCH)]
        + [pl.BlockSpec((None, N_BRANCH, BRANCH_W, tn), lambda j, i: (layer, 0, 0, j))],
        out_specs=pl.BlockSpec((tm, tn), lambda j, i: (i, j)),
        scratch_shapes=[pltpu.VMEM((N_BRANCH, D_MODEL, tn), BF16),
                        pltpu.VMEM((N_BRANCH, BRANCH_W, tn), BF16)],
        compiler_params=_params("arbitrary", "arbitrary"),
        name="merge",
    )(h, *ctx_branches, *lat_branches, *([w_mgate] * N_BRANCH),
      *([b_mgate.reshape(DEPTH, 1, N_BRANCH * D_MODEL)] * N_BRANCH), w_branch)


def _router_kernel(h_ref, w_ref, b_ref, idx_ref, wgt_ref, rank_ref, cnt_ref, run_ref):
    @pl.when(pl.program_id(0) == 0)
    def _():
        run_ref[...] = jnp.zeros(run_ref.shape, F32)

    logits = jnp.dot(h_ref[...].astype(BF16), w_ref[...].astype(BF16),
                     preferred_element_type=F32) + b_ref[...]
    lane = lax.broadcasted_iota(jnp.int32, logits.shape, 1).astype(F32)
    neg = jnp.float32(-jnp.inf)
    logits = jnp.where(lane < N_EXPERTS, logits, neg)
    idx_out = jnp.zeros(logits.shape, F32)
    val_out = jnp.zeros(logits.shape, F32)
    chosen = jnp.zeros(logits.shape, F32)
    picks = []
    top = None
    denom = None
    for k in range(TOP_K):
        m = jnp.max(logits, axis=-1, keepdims=True)
        idx = jnp.min(jnp.where(logits == m, lane, float(LANE)), axis=-1, keepdims=True)
        if k == 0:
            top = m
        e = jnp.exp(m - top)
        denom = e if k == 0 else denom + e
        idx_out = jnp.where(lane == k, idx, idx_out)
        val_out = jnp.where(lane == k, e, val_out)
        hit = lane == idx
        chosen = jnp.where(hit, 1.0, chosen)
        logits = jnp.where(hit, neg, logits)
        picks.append(hit)
    idx_ref[...] = idx_out.astype(jnp.int32)
    wgt_ref[...] = val_out / denom

    tm = logits.shape[0]
    earlier = (lax.broadcasted_iota(jnp.int32, (tm, tm), 1)
               < lax.broadcasted_iota(jnp.int32, (tm, tm), 0))
    before = jnp.dot(jnp.where(earlier, 1.0, 0.0).astype(BF16), chosen.astype(BF16),
                     preferred_element_type=F32) + run_ref[0:1, :]
    rank_out = jnp.zeros(logits.shape, F32)
    for k in range(TOP_K):
        r = jnp.sum(jnp.where(picks[k], before, 0.0), axis=-1, keepdims=True)
        rank_out = jnp.where(lane == k, r, rank_out)
    rank_ref[...] = rank_out.astype(jnp.int32)
    total = run_ref[0:1, :] + jnp.sum(chosen, axis=0, keepdims=True)
    run_ref[...] = jnp.broadcast_to(total, run_ref.shape)
    cnt_ref[...] = jnp.broadcast_to(total, cnt_ref.shape)


def _router(h, w_router_pad, b_router_pad, layer):
    row = pl.BlockSpec((ROW_TILE, LANE), lambda i: (i, 0))
    return pl.pallas_call(
        _router_kernel,
        out_shape=(jax.ShapeDtypeStruct((T_ALL, LANE), jnp.int32),
                   jax.ShapeDtypeStruct((T_ALL, LANE), F32),
                   jax.ShapeDtypeStruct((T_ALL, LANE), jnp.int32),
                   jax.ShapeDtypeStruct((8, LANE), F32)),
        grid=(T_ALL // ROW_TILE,),
        in_specs=[pl.BlockSpec((ROW_TILE, D_MODEL), lambda i: (i, 0)),
                  pl.BlockSpec((None, D_MODEL, LANE), lambda i: (layer, 0, 0)),
                  pl.BlockSpec((None, 1, LANE), lambda i: (layer, 0, 0))],
        out_specs=(row, row, row, pl.BlockSpec((8, LANE), lambda i: (0, 0))),
        scratch_shapes=[pltpu.VMEM((8, LANE), F32)],
        compiler_params=_params("arbitrary"),
        name="router",
    )(h, w_router_pad, b_router_pad)


def _expert_changed(tile_e_ref):
    m = pl.program_id(1)
    prev = tile_e_ref[jnp.maximum(m - 1, 0)]
    return jnp.logical_or(m == 0, tile_e_ref[m] != prev)


def _expert_up_kernel(tile_e_ref, nvalid_ref, x_ref, wg_ref, wl_ref, bg_ref, bl_ref, a_ref,
                      wgb_ref, wlb_ref):
    @pl.when(_expert_changed(tile_e_ref))
    def _():
        _cache_bf16(wg_ref, wgb_ref)
        _cache_bf16(wl_ref, wlb_ref)

    @pl.when(pl.program_id(1) < nvalid_ref[0])
    def _():
        x = x_ref[...].astype(BF16)
        gate = jnp.dot(x, wgb_ref[...], preferred_element_type=F32) + bg_ref[...]
        lin = jnp.dot(x, wlb_ref[...], preferred_element_type=F32) + bl_ref[...]
        gate = jnp.minimum(gate, SWIGLU_LIMIT)
        lin = jnp.clip(lin, -SWIGLU_LIMIT, SWIGLU_LIMIT)
        a_ref[...] = ((lin + 1.0) * gate * jax.nn.sigmoid(SWIGLU_ALPHA * gate)).astype(a_ref.dtype)

    @pl.when(pl.program_id(1) >= nvalid_ref[0])
    def _():
        a_ref[...] = jnp.zeros(a_ref.shape, a_ref.dtype)


def _expert_down_kernel(tile_e_ref, nvalid_ref, a_ref, w_ref, b_ref, y_ref, wb_ref):
    @pl.when(_expert_changed(tile_e_ref))
    def _():
        _cache_bf16(w_ref, wb_ref)

    @pl.when(pl.program_id(1) < nvalid_ref[0])
    def _():
        y_ref[...] = jnp.dot(a_ref[...], wb_ref[...], preferred_element_type=F32) + b_ref[...]

    @pl.when(pl.program_id(1) >= nvalid_ref[0])
    def _():
        y_ref[...] = jnp.zeros(y_ref.shape, y_ref.dtype)


def _experts(xs, tile_e, nvalid, w_gu, b_gu, w_down, b_down, layer):
    n_f = D_EXPERT // MOE_TF
    act = pl.pallas_call(
        _expert_up_kernel,
        out_shape=jax.ShapeDtypeStruct((P_PAD, D_EXPERT), BF16),
        grid_spec=pltpu.PrefetchScalarGridSpec(
            num_scalar_prefetch=2,
            grid=(n_f, N_MOE_TILES),
            in_specs=[
                pl.BlockSpec((MOE_TM, D_MODEL),
                             lambda j, m, te, nv: (jnp.minimum(m, nv[0] - 1), 0)),
                pl.BlockSpec((None, None, D_MODEL, MOE_TF), lambda j, m, te, nv: (layer, te[m], 0, j)),
                pl.BlockSpec((None, None, D_MODEL, MOE_TF),
                             lambda j, m, te, nv: (layer, te[m], 0, n_f + j)),
                pl.BlockSpec((None, None, 1, MOE_TF), lambda j, m, te, nv: (layer, te[m], 0, j)),
                pl.BlockSpec((None, None, 1, MOE_TF),
                             lambda j, m, te, nv: (layer, te[m], 0, n_f + j)),
            ],
            out_specs=pl.BlockSpec((MOE_TM, MOE_TF), lambda j, m, te, nv: (m, j)),
            scratch_shapes=[pltpu.VMEM((D_MODEL, MOE_TF), BF16), pltpu.VMEM((D_MODEL, MOE_TF), BF16)],
        ),
        compiler_params=_params("arbitrary", "arbitrary"),
        name="expert_up",
    )(tile_e, nvalid, xs, w_gu, w_gu, b_gu.reshape(DEPTH, N_EXPERTS, 1, 2 * D_EXPERT),
      b_gu.reshape(DEPTH, N_EXPERTS, 1, 2 * D_EXPERT))
    return pl.pallas_call(
        _expert_down_kernel,
        out_shape=jax.ShapeDtypeStruct((P_PAD, D_MODEL), F32),
        grid_spec=pltpu.PrefetchScalarGridSpec(
            num_scalar_prefetch=2,
            grid=(D_MODEL // MOE_TN, N_MOE_TILES),
            in_specs=[
                pl.BlockSpec((MOE_TM, D_EXPERT), lambda j, m, te, nv: (m, 0)),
                pl.BlockSpec((None, None, D_EXPERT, MOE_TN), lambda j, m, te, nv: (layer, te[m], 0, j)),
                pl.BlockSpec((None, None, 1, MOE_TN), lambda j, m, te, nv: (layer, te[m], 0, j)),
            ],
            out_specs=pl.BlockSpec((MOE_TM, MOE_TN), lambda j, m, te, nv: (m, j)),
            scratch_shapes=[pltpu.VMEM((D_EXPERT, MOE_TN), BF16)],
        ),
        compiler_params=_params("arbitrary", "arbitrary"),
        name="expert_down",
    )(tile_e, nvalid, act, w_down, b_down.reshape(DEPTH, N_EXPERTS, 1, D_MODEL))


def _dispatch_plan(idx, rank, cnt):
    counts = cnt[0, :N_EXPERTS].astype(jnp.int32)
    padded = ((counts + MOE_TM - 1) // MOE_TM) * MOE_TM
    gend = jnp.cumsum(padded)
    gstart = gend - padded
    top_i = idx[:, :TOP_K]
    experts = jnp.arange(N_EXPERTS, dtype=jnp.int32)
    start_of = jnp.sum(jnp.where(top_i[:, :, None] == experts, gstart, 0), axis=-1)
    dest = (start_of + rank[:, :TOP_K]).reshape(-1)
    tile_start = jnp.arange(N_MOE_TILES, dtype=jnp.int32) * MOE_TM
    tile_e = jnp.minimum(jnp.sum((gend[None, :] <= tile_start[:, None]).astype(jnp.int32), axis=1),
                         N_EXPERTS - 1)
    nvalid = gend[-1:] // MOE_TM
    last_tile = gend // MOE_TM - 1
    return dest, tile_e, nvalid, last_tile, padded


def _row_copies(tile, dest_ref, src_row, dst_row, sem):
    base = tile * (ROW_TILE * TOP_K)

    def body(r, carry):
        for k in range(TOP_K):
            d = dest_ref[base + r * TOP_K + k]
            pltpu.make_async_copy(src_row(r, k, d), dst_row(r, k, d), sem).start(priority=k % 2)
        return carry

    lax.fori_loop(0, ROW_TILE, body, 0, unroll=8)


def _dispatch_kernel(dest_ref, last_tile_ref, padded_ref, nvalid_ref, h_ref, xs_ref, zero_ref,
                     sem, zsem):
    def zero_tile(t):
        rows = pl.ds(pl.multiple_of(t * MOE_TM, MOE_TM), MOE_TM)
        return pltpu.make_async_copy(zero_ref, xs_ref.at[rows], zsem)

    @pl.when(pl.program_id(0) == 0)
    def _():
        zero_ref[...] = jnp.zeros(zero_ref.shape, zero_ref.dtype)
        for do in ("start", "wait"):
            for e in range(N_EXPERTS):
                @pl.when(padded_ref[e] > 0)
                def _():
                    getattr(zero_tile(last_tile_ref[e]), do)()
            for t in range(P_PAIRS // MOE_TM, N_MOE_TILES):
                @pl.when(t >= nvalid_ref[0])
                def _():
                    getattr(zero_tile(t), do)()

    _row_copies(pl.program_id(0), dest_ref,
                lambda r, k, d: h_ref.at[pl.ds(r, 1)],
                lambda r, k, d: xs_ref.at[pl.ds(d, 1)], sem)
    for _ in range(TOP_K):
        pltpu.make_async_copy(h_ref, xs_ref.at[pl.ds(0, ROW_TILE)], sem).wait()


def _dispatch(h, dest, last_tile, padded, nvalid):
    return pl.pallas_call(
        _dispatch_kernel,
        out_shape=jax.ShapeDtypeStruct((P_PAD, D_MODEL), F32),
        grid_spec=pltpu.PrefetchScalarGridSpec(
            num_scalar_prefetch=4,
            grid=(T_ALL // ROW_TILE,),
            in_specs=[pl.BlockSpec((ROW_TILE, D_MODEL), lambda i, *_: (i, 0))],
            out_specs=pl.BlockSpec(memory_space=pl.ANY),
            scratch_shapes=[pltpu.VMEM((MOE_TM, D_MODEL), F32),
                            pltpu.SemaphoreType.DMA(()), pltpu.SemaphoreType.DMA(())],
        ),
        compiler_params=_params("arbitrary"),
        name="dispatch",
    )(dest, last_tile, padded, nvalid, h)


def _combine_kernel(dest_ref, x_ref, wgt_ref, g_ref, lg_ref, lb_ref, sc_ref, sh_ref, y_ref,
                    xo_ref, h_ref, ybuf_ref, sem):
    i = pl.program_id(0)

    def gather(tile, slot):
        _row_copies(tile, dest_ref,
                    lambda r, k, d: y_ref.at[pl.ds(d, 1)],
                    lambda r, k, d: ybuf_ref.at[slot, pl.ds(k * ROW_TILE + r, 1)],
                    sem.at[slot])

    @pl.when(i == 0)
    def _():
        gather(0, 0)

    @pl.when(i + 1 < pl.num_programs(0))
    def _():
        gather(i + 1, (i + 1) % 2)

    slot = i % 2
    pltpu.make_async_copy(y_ref.at[pl.ds(0, TOP_K * ROW_TILE)], ybuf_ref.at[slot],
                          sem.at[slot]).wait()
    wgt = wgt_ref[...]
    moe = jnp.zeros(x_ref.shape, F32)
    for k in range(TOP_K):
        moe = moe + wgt[:, k:k + 1] * ybuf_ref[slot, k * ROW_TILE:(k + 1) * ROW_TILE, :]
    z = DEEP_ALPHA * x_ref[...] + g_ref[...] * moe
    xn = _ln(z) * lg_ref[...] + lb_ref[...]
    xo_ref[...] = xn
    h_ref[...] = (_ln(xn) * (1.0 + sc_ref[...]) + sh_ref[...]).astype(h_ref.dtype)


def _combine_resid_ln(x, y, dest, wgt, mod_gate, gate_idx, ln_g, ln_b, layer, mod_next, sc_idx,
                      sh_idx):
    row = pl.BlockSpec((ROW_TILE, D_MODEL), lambda i, *_: (i, 0))
    vec = pl.BlockSpec((None, 1, D_MODEL), lambda i, *_: (layer, 0, 0))
    return pl.pallas_call(
        _combine_kernel,
        out_shape=(jax.ShapeDtypeStruct((T_ALL, D_MODEL), F32),
                   jax.ShapeDtypeStruct((T_ALL, D_MODEL), BF16)),
        grid_spec=pltpu.PrefetchScalarGridSpec(
            num_scalar_prefetch=1,
            grid=(T_ALL // ROW_TILE,),
            in_specs=[row, pl.BlockSpec((ROW_TILE, LANE), lambda i, *_: (i, 0)),
                      _mod_spec(gate_idx), vec, vec, _mod_spec(sc_idx), _mod_spec(sh_idx),
                      pl.BlockSpec(memory_space=pl.ANY)],
            out_specs=(row, row),
            scratch_shapes=[pltpu.VMEM((2, TOP_K * ROW_TILE, D_MODEL), F32),
                            pltpu.SemaphoreType.DMA((2,))],
        ),
        compiler_params=_params("arbitrary"),
        name="combine_resid_ln",
    )(dest, x, wgt, mod_gate, ln_g.reshape(DEPTH, 1, D_MODEL), ln_b.reshape(DEPTH, 1, D_MODEL),
      mod_next, mod_next, y)


def kernel(x_prompt, x_sample, c, cache_k, cache_v, state_ret, c_ctx, w_mod, b_mod, w_in,
           ret_decay_logit, diff_lambda, conv_w, w_branch, w_mgate, b_mgate, w_out,
           ln1_g, ln1_b, ln2_g, ln2_b, w_router, b_router, w_gu, b_gu, w_down, b_down):
    x = jnp.concatenate([x_prompt.reshape(T_PROMPT, D_MODEL), x_sample.reshape(T_SAMPLE, D_MODEL)], 0)
    cond = jnp.concatenate([c_ctx[None, :], c, jnp.zeros((COND_PAD - N_COND, D_MODEL), F32)], 0)
    ret_tables = _rope_tables(RET_DK)
    diff_tables = _rope_tables(DIFF_DH)
    cw, sw = (t.astype(BF16) for t in _dft_tables(LANE))
    cs_p = jnp.concatenate(_dft_tables(SEQ), axis=1).astype(BF16)
    cs_s = jnp.concatenate(_dft_tables(DEC_SEQ), axis=1).astype(BF16)
    conv_w_t = jnp.swapaxes(conv_w, 1, 2)
    w_router_pad = jnp.pad(w_router, ((0, 0), (0, 0), (0, LANE - N_EXPERTS)))
    b_router_pad = jnp.pad(b_router, ((0, 0), (0, LANE - N_EXPERTS))).reshape(DEPTH, 1, LANE)

    mods = [_modulation(cond, w_mod, b_mod, l).reshape(COND_PAD, 6, 1, D_MODEL)
            for l in range(DEPTH)]
    ks_out, vs_out, ss_out = [], [], []
    h = _ln_mod(x, mods[0], 1, 0)
    for l in range(DEPTH):
        lam_init = 0.8 - 0.6 * math.exp(-0.3 * l)
        proj = _matmul(h, w_in, l, F32, IN_TN, "in_proj")
        proj_p = proj[:T_PROMPT]
        ks_out.append(proj_p[:, COL_DK * LANE:COL_DV * LANE]
                      .reshape(BATCH, SEQ, DIFF_HEADS, 2, DIFF_DH))
        vs_out.append(proj_p[:, COL_DV * LANE:COL_CB * LANE]
                      .reshape(BATCH, SEQ, DIFF_HEADS, 2 * DIFF_DH))
        ret_p, s_fin = _retention(proj, ret_decay_logit, l, use_ctx=False)
        ret_s = _retention(proj, ret_decay_logit, l, use_ctx=True, state_ret=state_ret,
                           tables=ret_tables)
        ss_out.append(s_fin)
        diff_p = _diff_attention(proj, diff_lambda, l, lam_init, use_ctx=False)
        diff_s = _diff_attention(proj, diff_lambda, l, lam_init, use_ctx=True, cache_k=cache_k,
                                 cache_v=cache_v, tables=diff_tables)
        conv_p = _gated_conv(proj, conv_w_t, l, use_ctx=False)
        conv_s = _gated_conv(proj, conv_w_t, l, use_ctx=True)
        four_p = _fourier(proj, cw, sw, cs_p, use_ctx=False)
        four_s = _fourier(proj, cw, sw, cs_s, use_ctx=True)
        merged = _merge(h, [ret_p, diff_p, conv_p, four_p], [ret_s, diff_s, conv_s, four_s],
                        w_mgate, b_mgate, w_branch, l)
        mix = _matmul(merged, w_out, l, F32, MM_TN, "out_proj")
        x, h2 = _resid_ln(x, mix, mods[l], 2, ln1_g, ln1_b, l, mods[l], 4, 3, F32)
        idx, wgt, rank, cnt = _router(h2, w_router_pad, b_router_pad, l)
        dest, tile_e, nvalid, last_tile, padded = _dispatch_plan(idx, rank, cnt)
        xs = _dispatch(h2, dest, last_tile, padded, nvalid)
        y = _experts(xs, tile_e, nvalid, w_gu, b_gu, w_down, b_down, l)
        nxt = mods[min(l + 1, DEPTH - 1)]
        x, h = _combine_resid_ln(x, y, dest, wgt, mods[l], 5, ln2_g, ln2_b, l, nxt, 1, 0)
    y_p = x[:T_PROMPT].reshape(BATCH, SEQ, D_MODEL)
    y_s = x[T_PROMPT:].reshape(DEC_BATCH, DEC_SEQ, D_MODEL)
    return (y_p, y_s, jnp.stack(ks_out, axis=1), jnp.stack(vs_out, axis=1),
            jnp.stack(ss_out, axis=1))
```

```python
import functools
import math

import jax
import jax.numpy as jnp
import numpy as np
from jax import lax
from jax.experimental import pallas as pl
from jax.experimental.pallas import tpu as pltpu

D_MODEL = 2048
BATCH = 16
SEQ = 256
DEPTH = 4
DEC_BATCH = 4
DEC_SEQ = 2048
PAST_LEN = 512
GRID_W = 64
N_BRANCH = 4
BRANCH_W = 512
RET_HEADS = 4
RET_DK = 128
RET_CHUNK = 128
DIFF_HEADS = 4
DIFF_DH = 64
N_EXPERTS = 32
TOP_K = 4
D_EXPERT = 2048
SWIGLU_ALPHA = 1.702
SWIGLU_LIMIT = 7.0
ROPE_BASE = 10000.0
LN_EPS = 1e-5
RMS_EPS = 1e-6
DEEP_ALPHA = (2 * DEPTH) ** 0.25
D_IN = 5632

LANE = 128
T_PROMPT = BATCH * SEQ
T_SAMPLE = DEC_BATCH * DEC_SEQ
T_ALL = T_PROMPT + T_SAMPLE
N_COND = 1 + DEC_BATCH
COND_PAD = 8

ROW_TILE = 256
MM_TM = 512
MM_TN = 512
IN_TN = 1408
MERGE_TN = 256
MOE_TM = 256
MOE_S = 8
MOE_FC = 256
N_CHUNK = D_EXPERT // MOE_FC
CAST_ROWS = 256
P_PAIRS = T_ALL * TOP_K
P_PAD = P_PAIRS + N_EXPERTS * MOE_TM
N_MOE_TILES = P_PAD // MOE_TM
G_MAX = N_EXPERTS + N_MOE_TILES // MOE_S
VMEM_LIMIT = 56 * 1024 * 1024

COL_RQ, COL_RK, COL_RV, COL_RG = 0, 4, 8, 12
COL_DQ, COL_DK, COL_DV = 16, 20, 24
COL_CB, COL_CC, COL_CX = 28, 32, 36
COL_FX = 40

BF16 = jnp.bfloat16
F32 = jnp.float32


def _params(*sem):
    return pltpu.CompilerParams(dimension_semantics=sem, vmem_limit_bytes=VMEM_LIMIT)


def _cond_row_of_tile(i, tile):
    start = i * tile
    return jnp.where(start < T_PROMPT, 0, 1 + (start - T_PROMPT) // DEC_SEQ)


def _cache_bf16(src_ref, dst_ref):
    n = src_ref.shape[0] // CAST_ROWS

    def body(i, carry):
        r = pl.multiple_of(i * CAST_ROWS, CAST_ROWS)
        dst_ref[pl.ds(r, CAST_ROWS), :] = src_ref[pl.ds(r, CAST_ROWS), :].astype(BF16)
        return carry

    lax.fori_loop(0, n, body, 0)


def _mod_kernel(c_ref, w_ref, b_ref, o_ref):
    c = c_ref[...]
    a = (c * jax.nn.sigmoid(c)).astype(BF16)
    o_ref[...] = jnp.dot(a, w_ref[...].astype(BF16), preferred_element_type=F32) + b_ref[...]


def _modulation(cond_pad, w_mod, b_mod, layer):
    n = 6 * D_MODEL
    return pl.pallas_call(
        _mod_kernel,
        out_shape=jax.ShapeDtypeStruct((COND_PAD, n), F32),
        grid=(n // MM_TN,),
        in_specs=[
            pl.BlockSpec((COND_PAD, D_MODEL), lambda j: (0, 0)),
            pl.BlockSpec((None, D_MODEL, MM_TN), lambda j: (layer, 0, j)),
            pl.BlockSpec((None, 1, MM_TN), lambda j: (layer, 0, j)),
        ],
        out_specs=pl.BlockSpec((COND_PAD, MM_TN), lambda j: (0, j)),
        compiler_params=_params("arbitrary"),
        name="modulation",
    )(cond_pad, w_mod, b_mod.reshape(DEPTH, 1, n))


def _ln(x):
    xc = x - jnp.mean(x, axis=-1, keepdims=True)
    return xc * lax.rsqrt(jnp.mean(xc * xc, axis=-1, keepdims=True) + LN_EPS)


def _ln_mod_kernel(x_ref, sc_ref, sh_ref, h_ref):
    h_ref[...] = (_ln(x_ref[...]) * (1.0 + sc_ref[...]) + sh_ref[...]).astype(h_ref.dtype)


def _mod_spec(which):
    return pl.BlockSpec((None, None, 1, D_MODEL),
                        lambda i, *_: (_cond_row_of_tile(i, ROW_TILE), which, 0, 0))


def _ln_mod(x, mod4, sc_idx, sh_idx):
    return pl.pallas_call(
        _ln_mod_kernel,
        out_shape=jax.ShapeDtypeStruct((T_ALL, D_MODEL), BF16),
        grid=(T_ALL // ROW_TILE,),
        in_specs=[pl.BlockSpec((ROW_TILE, D_MODEL), lambda i: (i, 0)),
                  _mod_spec(sc_idx), _mod_spec(sh_idx)],
        out_specs=pl.BlockSpec((ROW_TILE, D_MODEL), lambda i: (i, 0)),
        compiler_params=_params("parallel"),
        name="ln_mod",
    )(x, mod4, mod4)


def _resid_ln_kernel(x_ref, y_ref, g_ref, lg_ref, lb_ref, sc_ref, sh_ref, xo_ref, h_ref):
    z = DEEP_ALPHA * x_ref[...] + g_ref[...] * y_ref[...]
    xn = _ln(z) * lg_ref[...] + lb_ref[...]
    xo_ref[...] = xn
    h_ref[...] = (_ln(xn) * (1.0 + sc_ref[...]) + sh_ref[...]).astype(h_ref.dtype)


def _resid_ln(x, y, mod_gate, gate_idx, ln_g, ln_b, layer, mod_next, sc_idx, sh_idx, h_dtype):
    row = pl.BlockSpec((ROW_TILE, D_MODEL), lambda i: (i, 0))
    vec = pl.BlockSpec((None, 1, D_MODEL), lambda i: (layer, 0, 0))
    return pl.pallas_call(
        _resid_ln_kernel,
        out_shape=(jax.ShapeDtypeStruct((T_ALL, D_MODEL), F32),
                   jax.ShapeDtypeStruct((T_ALL, D_MODEL), h_dtype)),
        grid=(T_ALL // ROW_TILE,),
        in_specs=[row, row, _mod_spec(gate_idx), vec, vec, _mod_spec(sc_idx), _mod_spec(sh_idx)],
        out_specs=(row, row),
        compiler_params=_params("parallel"),
        name="resid_ln",
    )(x, y, mod_gate, ln_g.reshape(DEPTH, 1, D_MODEL), ln_b.reshape(DEPTH, 1, D_MODEL),
      mod_next, mod_next)


def _mm_kernel(x_ref, w_ref, o_ref, wb_ref):
    @pl.when(pl.program_id(1) == 0)
    def _():
        _cache_bf16(w_ref, wb_ref)

    o_ref[...] = jnp.dot(x_ref[...], wb_ref[...], preferred_element_type=F32).astype(o_ref.dtype)


def _matmul(x, w, layer, out_dtype, tn, name):
    m, k = x.shape
    n = w.shape[-1]
    return pl.pallas_call(
        _mm_kernel,
        out_shape=jax.ShapeDtypeStruct((m, n), out_dtype),
        grid=(n // tn, m // MM_TM),
        in_specs=[pl.BlockSpec((MM_TM, k), lambda j, i: (i, 0)),
                  pl.BlockSpec((None, k, tn), lambda j, i: (layer, 0, j))],
        out_specs=pl.BlockSpec((MM_TM, tn), lambda j, i: (i, j)),
        scratch_shapes=[pltpu.VMEM((k, tn), BF16)],
        compiler_params=_params("arbitrary", "arbitrary"),
        name=name,
    )(x, w)


def _rope_tables(d):
    rows = DEC_SEQ // GRID_W
    row = jnp.repeat(jnp.arange(rows, dtype=F32), GRID_W)
    col = jnp.tile(jnp.arange(GRID_W, dtype=F32), rows)
    axis_dim = d // 2
    inv_freq = ROPE_BASE ** (-jnp.arange(0, axis_dim, 2, dtype=F32) / axis_dim)
    ang_r = row[:, None] * inv_freq
    ang_c = col[:, None] * inv_freq
    cos = jnp.concatenate([jnp.cos(ang_r), jnp.cos(ang_r), jnp.cos(ang_c), jnp.cos(ang_c)], -1)
    sin = jnp.concatenate([-jnp.sin(ang_r), jnp.sin(ang_r), -jnp.sin(ang_c), jnp.sin(ang_c)], -1)
    reps = LANE // d
    return jnp.tile(cos, (1, reps)), jnp.tile(sin, (1, reps))


def _rope(x, cos, sin, d):
    q = d // 4
    lane = lax.broadcasted_iota(jnp.int32, x.shape, 1)
    first = (lane % (2 * q)) < q
    partner = jnp.where(first, pltpu.roll(x, LANE - q, 1), pltpu.roll(x, q, 1))
    return x * cos + partner * sin


def _retention_kernel(*refs, seq, use_ctx):
    if use_ctx:
        (q_ref, k_ref, v_ref, g_ref, lgt_ref, s0_ref, cos_ref, sin_ref, o_ref,
         of_ref, ob_ref) = refs
    else:
        q_ref, k_ref, v_ref, g_ref, lgt_ref, o_ref, sfin_ref, of_ref, ob_ref = refs
    c = RET_CHUNK
    n_chunk = seq // c
    ii = lax.broadcasted_iota(jnp.int32, (c, c), 0).astype(F32)
    jj = lax.broadcasted_iota(jnp.int32, (c, c), 1).astype(F32)
    col_pos = lax.broadcasted_iota(jnp.int32, (c, 1), 0).astype(F32)
    row_pos = lax.broadcasted_iota(jnp.int32, (1, c), 1).astype(F32)

    def direction(d):
        logit = lgt_ref[d]
        lg = jnp.minimum(logit, 0.0) - jnp.log1p(jnp.exp(-jnp.abs(logit)))
        lg_mat = jnp.broadcast_to(lg[0:1, :], (c, c))
        lg_col = lg[:, 0:1][0:1, :]
        if d == 0:
            dist = ii - jj
            kw = jnp.exp((c - 1 - row_pos) * lg_col)
            qw = jnp.exp((col_pos + 1.0) * lg_col)
        else:
            dist = jj - ii
            kw = jnp.exp(row_pos * lg_col)
            qw = jnp.exp((c - col_pos) * lg_col)
        decay = jnp.where(dist >= 0, jnp.exp(jnp.where(dist >= 0, dist, 0.0) * lg_mat), 0.0)
        chunk_decay = jnp.exp(c * lg_mat)
        return decay, kw, qw, chunk_decay

    def chunk(n):
        rows = slice(n * c, (n + 1) * c)
        q = q_ref[rows, :]
        k = k_ref[rows, :] * (RET_DK ** -0.5)
        if use_ctx:
            q = _rope(q, cos_ref[rows, :], sin_ref[rows, :], RET_DK)
            k = _rope(k, cos_ref[rows, :], sin_ref[rows, :], RET_DK)
        return q, k.T, v_ref[rows, :].astype(BF16)

    for d, acc_ref in ((0, of_ref), (1, ob_ref)):
        decay, kw, qw, chunk_decay = direction(d)
        if use_ctx:
            state = s0_ref[d]
        else:
            state = jnp.zeros((RET_DK, RET_DK), F32)
        order = range(n_chunk) if d == 0 else range(n_chunk - 1, -1, -1)
        for n in order:
            q, kt, v = chunk(n)
            scores = jnp.dot(q.astype(BF16), kt.astype(BF16), preferred_element_type=F32) * decay
            o = jnp.dot(scores.astype(BF16), v, preferred_element_type=F32)
            o = o + jnp.dot((q * qw).astype(BF16), state.astype(BF16), preferred_element_type=F32)
            acc_ref[n * c:(n + 1) * c, :] = o
            state = chunk_decay * state + jnp.dot((kt * kw).astype(BF16), v,
                                                  preferred_element_type=F32)
        if not use_ctx:
            sfin_ref[d] = state

    o = of_ref[...] + ob_ref[...]
    o = o * lax.rsqrt(jnp.mean(o * o, axis=-1, keepdims=True) + RMS_EPS)
    g = g_ref[...]
    o_ref[...] = (g * jax.nn.sigmoid(g) * o).astype(o_ref.dtype)


def _retention(proj, decay_logit, layer, *, use_ctx, state_ret=None, tables=None):
    seq, nb, row0 = (DEC_SEQ, DEC_BATCH, T_PROMPT // DEC_SEQ) if use_ctx else (SEQ, BATCH, 0)

    def col(c0):
        return pl.BlockSpec((seq, LANE), lambda b, h: (row0 + b, c0 + h))

    lgt = jnp.broadcast_to(decay_logit[layer][:, :, None, None], (2, RET_HEADS, 8, LANE))
    in_specs = [col(COL_RQ), col(COL_RK), col(COL_RV), col(COL_RG),
                pl.BlockSpec((2, None, 8, LANE), lambda b, h: (0, h, 0, 0))]
    args = [proj, proj, proj, proj, lgt]
    out_branch = jax.ShapeDtypeStruct((nb * seq, BRANCH_W), BF16)
    branch_spec = pl.BlockSpec((seq, LANE), lambda b, h: (b, h))
    if use_ctx:
        in_specs += [pl.BlockSpec((None, None, 2, None, RET_DK, RET_DK),
                                  lambda b, h: (b, layer, 0, h, 0, 0)),
                     pl.BlockSpec((seq, LANE), lambda b, h: (0, 0)),
                     pl.BlockSpec((seq, LANE), lambda b, h: (0, 0))]
        args += [state_ret, tables[0], tables[1]]
        out_shape, out_specs = out_branch, branch_spec
    else:
        out_shape = (out_branch,
                     jax.ShapeDtypeStruct((nb, 2, RET_HEADS, RET_DK, RET_DK), F32))
        out_specs = (branch_spec,
                     pl.BlockSpec((None, 2, None, RET_DK, RET_DK), lambda b, h: (b, 0, h, 0, 0)))
    return pl.pallas_call(
        functools.partial(_retention_kernel, seq=seq, use_ctx=use_ctx),
        out_shape=out_shape,
        grid=(nb, RET_HEADS),
        in_specs=in_specs,
        out_specs=out_specs,
        scratch_shapes=[pltpu.VMEM((seq, LANE), F32), pltpu.VMEM((seq, LANE), F32)],
        compiler_params=_params("parallel", "parallel"),
        name="retention_ctx" if use_ctx else "retention",
    )(*args)


def _diff_attn_kernel(*refs, seq, tq, use_ctx, lam_init):
    if use_ctx:
        (q_ref, k_ref, v_ref, lam_ref, kc_ref, vc_ref, cosq_ref, sinq_ref, cosk_ref, sink_ref,
         o_ref, kt_ref, va_ref) = refs
    else:
        q_ref, k_ref, v_ref, lam_ref, o_ref, kt_ref, va_ref = refs
    c = LANE

    @pl.when(pl.program_id(2) == 0)
    def _():
        for n in range(seq // c):
            rows = slice(n * c, (n + 1) * c)
            k = k_ref[rows, :]
            if use_ctx:
                k = _rope(k, cosk_ref[rows, :], sink_ref[rows, :], DIFF_DH)
            kt_ref[:, rows] = k.T.astype(BF16)
        va_ref[0:seq, :] = v_ref[...].astype(BF16)
        if use_ctx:
            for n in range(PAST_LEN // c):
                rows = slice(n * c, (n + 1) * c)
                kt_ref[:, seq + n * c:seq + (n + 1) * c] = kc_ref[rows, :].T.astype(BF16)
            va_ref[seq:seq + PAST_LEN, :] = vc_ref[...].astype(BF16)

    lam_v = lam_ref[...]
    lam = (jnp.exp(jnp.sum(lam_v[0:1] * lam_v[1:2], axis=-1, keepdims=True))
           - jnp.exp(jnp.sum(lam_v[2:3] * lam_v[3:4], axis=-1, keepdims=True)) + lam_init)
    q = q_ref[...]
    if use_ctx:
        q = _rope(q, cosq_ref[...], sinq_ref[...], DIFF_DH)
    lane = lax.broadcasted_iota(jnp.int32, q.shape, 1)
    q = q * (DIFF_DH ** -0.5)
    kt = kt_ref[...]
    probs = []
    for m in range(2):
        qm = jnp.where((lane < DIFF_DH) == (m == 0), q, 0.0).astype(BF16)
        s = jnp.dot(qm, kt, preferred_element_type=F32)
        p = jnp.exp(s - jnp.max(s, axis=-1, keepdims=True))
        inv = 1.0 / jnp.sum(p, axis=-1, keepdims=True)
        probs.append((p, inv))
    w = probs[0][0] * probs[0][1] - probs[1][0] * (lam * probs[1][1])
    o = jnp.dot(w.astype(BF16), va_ref[...], preferred_element_type=F32)
    o = o * lax.rsqrt(jnp.mean(o * o, axis=-1, keepdims=True) + RMS_EPS) * (1.0 - lam_init)
    o_ref[...] = o.astype(o_ref.dtype)


def _diff_attention(proj, diff_lambda, layer, lam_init, *, use_ctx, cache_k=None, cache_v=None,
                    tables=None):
    seq, nb, row0 = (DEC_SEQ, DEC_BATCH, T_PROMPT // DEC_SEQ) if use_ctx else (SEQ, BATCH, 0)
    tq = 256
    n_q = seq // tq
    n_keys = seq + (PAST_LEN if use_ctx else 0)
    in_specs = [pl.BlockSpec((tq, LANE), lambda b, h, i: ((row0 + b) * n_q + i, COL_DQ + h)),
                pl.BlockSpec((seq, LANE), lambda b, h, i: (row0 + b, COL_DK + h)),
                pl.BlockSpec((seq, LANE), lambda b, h, i: (row0 + b, COL_DV + h)),
                pl.BlockSpec((None, 4, DIFF_DH), lambda b, h, i: (layer, 0, 0))]
    args = [proj, proj, proj, diff_lambda]
    if use_ctx:
        ctx_spec = pl.BlockSpec((None, None, PAST_LEN, LANE), lambda b, h, i: (b, layer, 0, h))
        full = pl.BlockSpec((seq, LANE), lambda b, h, i: (0, 0))
        qtab = pl.BlockSpec((tq, LANE), lambda b, h, i: (i, 0))
        in_specs += [ctx_spec, ctx_spec, qtab, qtab, full, full]
        args += [cache_k.reshape(DEC_BATCH, DEPTH, PAST_LEN, DIFF_HEADS * 2 * DIFF_DH),
                 cache_v.reshape(DEC_BATCH, DEPTH, PAST_LEN, DIFF_HEADS * 2 * DIFF_DH),
                 tables[0], tables[1], tables[0], tables[1]]
    return pl.pallas_call(
        functools.partial(_diff_attn_kernel, seq=seq, tq=tq, use_ctx=use_ctx, lam_init=lam_init),
        out_shape=jax.ShapeDtypeStruct((nb * seq, BRANCH_W), BF16),
        grid=(nb, DIFF_HEADS, n_q),
        in_specs=in_specs,
        out_specs=pl.BlockSpec((tq, LANE), lambda b, h, i: (b * n_q + i, h)),
        scratch_shapes=[pltpu.VMEM((LANE, n_keys), BF16), pltpu.VMEM((n_keys, LANE), BF16)],
        compiler_params=_params("parallel", "parallel", "arbitrary"),
        name="diff_attention_ctx" if use_ctx else "diff_attention",
    )(*args)


def _conv_kernel(cb_ref, cc_ref, cx_ref, w_ref, o_ref, *, seq):
    u = cc_ref[...] * cx_ref[...]
    row = lax.broadcasted_iota(jnp.int32, u.shape, 0)
    prev = jnp.where(row == 0, 0.0, pltpu.roll(u, 1, 0))
    nxt = jnp.where(row == seq - 1, 0.0, pltpu.roll(u, seq - 1, 0))
    w = w_ref[...]
    conv = prev * w[0:1, :] + u * w[1:2, :] + nxt * w[2:3, :]
    o_ref[...] = (cb_ref[...] * conv).astype(o_ref.dtype)


def _gated_conv(proj, conv_w_t, layer, *, use_ctx):
    seq, nb, row0 = (DEC_SEQ, DEC_BATCH, T_PROMPT // DEC_SEQ) if use_ctx else (SEQ, BATCH, 0)
    n_c = BRANCH_W // LANE

    def col(c0):
        return pl.BlockSpec((seq, LANE), lambda b, c: (row0 + b, c0 + c))

    return pl.pallas_call(
        functools.partial(_conv_kernel, seq=seq),
        out_shape=jax.ShapeDtypeStruct((nb * seq, BRANCH_W), BF16),
        grid=(nb, n_c),
        in_specs=[col(COL_CB), col(COL_CC), col(COL_CX),
                  pl.BlockSpec((None, 3, LANE), lambda b, c: (layer, 0, c))],
        out_specs=pl.BlockSpec((seq, LANE), lambda b, c: (b, c)),
        compiler_params=_params("parallel", "parallel"),
        name="gated_conv_ctx" if use_ctx else "gated_conv",
    )(proj, proj, proj, conv_w_t)


def _dft_tables(n):
    j = jnp.arange(n, dtype=jnp.int32)
    m = (j[:, None] * j[None, :]) % n
    ang = m.astype(F32) * (2.0 * math.pi / n)
    return jnp.cos(ang), jnp.sin(ang)


def _fourier_kernel(x_ref, cw_ref, sw_ref, cs_ref, o_ref, y_ref, *, seq):
    @pl.when(pl.program_id(1) == 0)
    def _():
        for g in range(BRANCH_W // LANE):
            cols = slice(g * LANE, (g + 1) * LANE)
            xg = x_ref[:, cols].astype(BF16)
            y_ref[0:seq, cols] = jnp.dot(xg, cw_ref[...], preferred_element_type=F32).astype(BF16)
            y_ref[seq:2 * seq, cols] = (-jnp.dot(xg, sw_ref[...],
                                                 preferred_element_type=F32)).astype(BF16)

    o = jnp.dot(cs_ref[...], y_ref[...], preferred_element_type=F32)
    o_ref[...] = (o * ((seq * LANE) ** -0.5)).astype(o_ref.dtype)


def _fourier(proj, cw, sw, cs_l, *, use_ctx):
    seq, nb, row0 = (DEC_SEQ, DEC_BATCH, T_PROMPT // DEC_SEQ) if use_ctx else (SEQ, BATCH, 0)
    tl = 256
    n_t = seq // tl
    wide = BRANCH_W // LANE
    return pl.pallas_call(
        functools.partial(_fourier_kernel, seq=seq),
        out_shape=jax.ShapeDtypeStruct((nb * seq, BRANCH_W), BF16),
        grid=(nb, n_t),
        in_specs=[pl.BlockSpec((seq, BRANCH_W), lambda b, i: (row0 + b, COL_FX // wide)),
                  pl.BlockSpec((LANE, LANE), lambda b, i: (0, 0)),
                  pl.BlockSpec((LANE, LANE), lambda b, i: (0, 0)),
                  pl.BlockSpec((tl, 2 * seq), lambda b, i: (i, 0))],
        out_specs=pl.BlockSpec((tl, BRANCH_W), lambda b, i: (b * n_t + i, 0)),
        scratch_shapes=[pltpu.VMEM((2 * seq, BRANCH_W), BF16)],
        compiler_params=_params("parallel", "arbitrary"),
        name="fourier_ctx" if use_ctx else "fourier",
    )(proj, cw, sw, cs_l)


def _merge_kernel(h_ref, *refs):
    ctx_refs = refs[0:N_BRANCH]
    lat_refs = refs[N_BRANCH:2 * N_BRANCH]
    wg_refs = refs[2 * N_BRANCH:3 * N_BRANCH]
    bg_refs = refs[3 * N_BRANCH:4 * N_BRANCH]
    wb_ref, o_ref, wgb_ref, wbb_ref = refs[4 * N_BRANCH:]
    is_ctx = pl.program_id(1) < T_PROMPT // MM_TM

    @pl.when(pl.program_id(1) == 0)
    def _():
        for n in range(N_BRANCH):
            _cache_bf16(wg_refs[n], wgb_ref.at[n])
            wbb_ref[n] = wb_ref[n].astype(BF16)

    h = h_ref[...]
    acc = jnp.zeros(o_ref.shape, F32)
    for n in range(N_BRANCH):
        gate = jax.nn.sigmoid(jnp.dot(h, wgb_ref[n], preferred_element_type=F32) + bg_refs[n][...])
        branch = jnp.where(is_ctx, ctx_refs[n][...], lat_refs[n][...])
        acc = acc + gate * jnp.dot(branch, wbb_ref[n], preferred_element_type=F32)
    o_ref[...] = acc.astype(o_ref.dtype)


def _merge(h, ctx_branches, lat_branches, w_mgate, b_mgate, w_branch, layer):
    tm, tn = MM_TM, MERGE_TN
    n_t = D_MODEL // tn
    n_ctx = T_PROMPT // tm
    ctx_spec = pl.BlockSpec((tm, BRANCH_W), lambda j, i: (jnp.minimum(i, n_ctx - 1), 0))
    lat_spec = pl.BlockSpec((tm, BRANCH_W), lambda j, i: (jnp.maximum(i - n_ctx, 0), 0))

    def gate_w(n):
        return pl.BlockSpec((None, D_MODEL, tn), lambda j, i: (layer, 0, n * n_t + j))

    def gate_b(n):
        return pl.BlockSpec((None, 1, tn), lambda j, i: (layer, 0, n * n_t + j))

    return pl.pallas_call(
        _merge_kernel,
        out_shape=jax.ShapeDtypeStruct((T_ALL, D_MODEL), BF16),
        grid=(n_t, T_ALL // tm),
        in_specs=[pl.BlockSpec((tm, D_MODEL), lambda j, i: (i, 0))]
        + [ctx_spec] * N_BRANCH + [lat_spec] * N_BRANCH
        + [gate_w(n) for n in range(N_BRANCH)]
        + [gate_b(n) for n in range(N_BRANCH)]
        + [pl.BlockSpec((None, N_BRANCH, BRANCH_W, tn), lambda j, i: (layer, 0, 0, j))],
        out_specs=pl.BlockSpec((tm, tn), lambda j, i: (i, j)),
        scratch_shapes=[pltpu.VMEM((N_BRANCH, D_MODEL, tn), BF16),
                        pltpu.VMEM((N_BRANCH, BRANCH_W, tn), BF16)],
        compiler_params=_params("arbitrary", "arbitrary"),
        name="merge",
    )(h, *ctx_branches, *lat_branches, *([w_mgate] * N_BRANCH),
      *([b_mgate.reshape(DEPTH, 1, N_BRANCH * D_MODEL)] * N_BRANCH), w_branch)


def _router_kernel(h_ref, w_ref, b_ref, idx_ref, wgt_ref, rank_ref, cnt_ref, run_ref):
    @pl.when(pl.program_id(0) == 0)
    def _():
        run_ref[...] = jnp.zeros(run_ref.shape, F32)

    logits = jnp.dot(h_ref[...].astype(BF16), w_ref[...].astype(BF16),
                     preferred_element_type=F32) + b_ref[...]
    lane = lax.broadcasted_iota(jnp.int32, logits.shape, 1).astype(F32)
    neg = jnp.float32(-jnp.inf)
    logits = jnp.where(lane < N_EXPERTS, logits, neg)
    idx_out = jnp.zeros(logits.shape, F32)
    val_out = jnp.zeros(logits.shape, F32)
    chosen = jnp.zeros(logits.shape, F32)
    picks = []
    top = None
    denom = None
    for k in range(TOP_K):
        m = jnp.max(logits, axis=-1, keepdims=True)
        idx = jnp.min(jnp.where(logits == m, lane, float(LANE)), axis=-1, keepdims=True)
        if k == 0:
            top = m
        e = jnp.exp(m - top)
        denom = e if k == 0 else denom + e
        idx_out = jnp.where(lane == k, idx, idx_out)
        val_out = jnp.where(lane == k, e, val_out)
        hit = lane == idx
        chosen = jnp.where(hit, 1.0, chosen)
        logits = jnp.where(hit, neg, logits)
        picks.append(hit)
    idx_ref[...] = idx_out.astype(jnp.int32)
    wgt_ref[...] = val_out / denom

    tm = logits.shape[0]
    earlier = (lax.broadcasted_iota(jnp.int32, (tm, tm), 1)
               < lax.broadcasted_iota(jnp.int32, (tm, tm), 0))
    before = jnp.dot(jnp.where(earlier, 1.0, 0.0).astype(BF16), chosen.astype(BF16),
                     preferred_element_type=F32) + run_ref[0:1, :]
    rank_out = jnp.zeros(logits.shape, F32)
    for k in range(TOP_K):
        r = jnp.sum(jnp.where(picks[k], before, 0.0), axis=-1, keepdims=True)
        rank_out = jnp.where(lane == k, r, rank_out)
    rank_ref[...] = rank_out.astype(jnp.int32)
    total = run_ref[0:1, :] + jnp.sum(chosen, axis=0, keepdims=True)
    run_ref[...] = jnp.broadcast_to(total, run_ref.shape)
    cnt_ref[...] = jnp.broadcast_to(total, cnt_ref.shape)


def _router(h, w_router_pad, b_router_pad, layer):
    row = pl.BlockSpec((ROW_TILE, LANE), lambda i: (i, 0))
    return pl.pallas_call(
        _router_kernel,
        out_shape=(jax.ShapeDtypeStruct((T_ALL, LANE), jnp.int32),
                   jax.ShapeDtypeStruct((T_ALL, LANE), F32),
                   jax.ShapeDtypeStruct((T_ALL, LANE), jnp.int32),
                   jax.ShapeDtypeStruct((8, LANE), F32)),
        grid=(T_ALL // ROW_TILE,),
        in_specs=[pl.BlockSpec((ROW_TILE, D_MODEL), lambda i: (i, 0)),
                  pl.BlockSpec((None, D_MODEL, LANE), lambda i: (layer, 0, 0)),
                  pl.BlockSpec((None, 1, LANE), lambda i: (layer, 0, 0))],
        out_specs=(row, row, row, pl.BlockSpec((8, LANE), lambda i: (0, 0))),
        scratch_shapes=[pltpu.VMEM((8, LANE), F32)],
        compiler_params=_params("arbitrary"),
        name="router",
    )(h, w_router_pad, b_router_pad)


def _expert_kernel(ge_ref, gt0_ref, gnt_ref, ng_ref, nvalid_ref, bgu_ref, bdn_ref, xs_ref, wgu_ref,
                   wdn_ref, y_ref, xland_ref, xbf_ref, acc_ref, wg_ref, wl_ref, wd_ref, wgb_ref,
                   wlb_ref, wdb_ref, out_ref, xsem, wsem, osem, zsem, *, layer):
    g = pl.program_id(0)
    e, t0, nt, ng = ge_ref[g], gt0_ref[g], gnt_ref[g], ng_ref[0]
    fc = MOE_FC

    def weight_copies(expert, c, slot):
        return (pltpu.make_async_copy(wgu_ref.at[layer, expert, :, pl.ds(c * fc, fc)],
                                      wg_ref.at[slot], wsem.at[slot]),
                pltpu.make_async_copy(wgu_ref.at[layer, expert, :, pl.ds(D_EXPERT + c * fc, fc)],
                                      wl_ref.at[slot], wsem.at[slot]),
                pltpu.make_async_copy(wdn_ref.at[layer, expert, pl.ds(c * fc, fc), :],
                                      wd_ref.at[slot], wsem.at[slot]))

    def tile_rows(m):
        return pl.ds(pl.multiple_of((t0 + m) * MOE_TM, MOE_TM), MOE_TM)

    def x_copy(m, slot):
        return pltpu.make_async_copy(xs_ref.at[tile_rows(m)], xland_ref.at[slot], xsem.at[slot])

    def y_copy(m, slot):
        return pltpu.make_async_copy(out_ref.at[slot], y_ref.at[tile_rows(m)], osem.at[slot])

    @pl.when(g == 0)
    def _():
        out_ref[0] = jnp.zeros(out_ref.shape[1:], out_ref.dtype)
        for do in ("start", "wait"):
            for t in range(P_PAIRS // MOE_TM, N_MOE_TILES):
                @pl.when(t >= nvalid_ref[0])
                def _():
                    getattr(pltpu.make_async_copy(out_ref.at[0], y_ref.at[pl.ds(t * MOE_TM, MOE_TM)],
                                                  zsem), do)()
        for cp in weight_copies(e, 0, 0):
            cp.start()

    @pl.when(g < ng)
    def _():
        x_copy(0, 0).start()
        for c in range(N_CHUNK):
            slot = c % 2
            for cp in weight_copies(e, c, slot):
                cp.wait()
            if c + 1 < N_CHUNK:
                for cp in weight_copies(e, c + 1, 1 - slot):
                    cp.start()
            else:
                @pl.when(g + 1 < ng)
                def _():
                    for cp in weight_copies(ge_ref[g + 1], 0, 0):
                        cp.start()
            _cache_bf16(wg_ref.at[slot], wgb_ref)
            _cache_bf16(wl_ref.at[slot], wlb_ref)
            wdb_ref[...] = wd_ref[slot].astype(BF16)
            bg = bgu_ref[:, c * fc:(c + 1) * fc]
            bl = bgu_ref[:, D_EXPERT + c * fc:D_EXPERT + (c + 1) * fc]

            def tile_body(m, carry, c=c, bg=bg, bl=bl):
                if c == 0:
                    xslot = m % 2
                    x_copy(m, xslot).wait()

                    @pl.when(m + 1 < nt)
                    def _():
                        x_copy(m + 1, 1 - xslot).start()

                    xbf_ref[m] = xland_ref[xslot].astype(BF16)
                x = xbf_ref[m]
                gate = jnp.dot(x, wgb_ref[...], preferred_element_type=F32) + bg
                lin = jnp.dot(x, wlb_ref[...], preferred_element_type=F32) + bl
                gate = jnp.minimum(gate, SWIGLU_LIMIT)
                lin = jnp.clip(lin, -SWIGLU_LIMIT, SWIGLU_LIMIT)
                act = ((lin + 1.0) * gate * jax.nn.sigmoid(SWIGLU_ALPHA * gate)).astype(BF16)
                part = jnp.dot(act, wdb_ref[...], preferred_element_type=F32)
                if c == 0:
                    acc_ref[m] = part + bdn_ref[...]
                elif c + 1 < N_CHUNK:
                    acc_ref[m] += part
                else:
                    oslot = m % 2

                    @pl.when(m >= 2)
                    def _():
                        y_copy(m - 2, oslot).wait()

                    out_ref[oslot] = acc_ref[m] + part
                    y_copy(m, oslot).start()
                return carry

            lax.fori_loop(0, nt, tile_body, 0)

        @pl.when(nt >= 2)
        def _():
            y_copy(nt - 2, nt % 2).wait()

        y_copy(nt - 1, (nt - 1) % 2).wait()


def _experts(xs, groups, nvalid, w_gu, b_gu, w_down, b_down, layer):
    g_expert, g_tile0, g_tiles, n_groups = groups
    any_spec = pl.BlockSpec(memory_space=pl.ANY)
    return pl.pallas_call(
        functools.partial(_expert_kernel, layer=layer),
        out_shape=jax.ShapeDtypeStruct((P_PAD, D_MODEL), F32),
        grid_spec=pltpu.PrefetchScalarGridSpec(
            num_scalar_prefetch=5,
            grid=(G_MAX,),
            in_specs=[
                pl.BlockSpec((None, None, 1, 2 * D_EXPERT), lambda g, ge, *_: (layer, ge[g], 0, 0)),
                pl.BlockSpec((None, None, 1, D_MODEL), lambda g, ge, *_: (layer, ge[g], 0, 0)),
                any_spec, any_spec, any_spec,
            ],
            out_specs=any_spec,
            scratch_shapes=[
                pltpu.VMEM((2, MOE_TM, D_MODEL), F32),
                pltpu.VMEM((MOE_S, MOE_TM, D_MODEL), BF16),
                pltpu.VMEM((MOE_S, MOE_TM, D_MODEL), F32),
                pltpu.VMEM((2, D_MODEL, MOE_FC), F32),
                pltpu.VMEM((2, D_MODEL, MOE_FC), F32),
                pltpu.VMEM((2, MOE_FC, D_MODEL), F32),
                pltpu.VMEM((D_MODEL, MOE_FC), BF16),
                pltpu.VMEM((D_MODEL, MOE_FC), BF16),
                pltpu.VMEM((MOE_FC, D_MODEL), BF16),
                pltpu.VMEM((2, MOE_TM, D_MODEL), F32),
                pltpu.SemaphoreType.DMA((2,)), pltpu.SemaphoreType.DMA((2,)),
                pltpu.SemaphoreType.DMA((2,)), pltpu.SemaphoreType.DMA(()),
            ],
        ),
        compiler_params=_params("arbitrary"),
        name="experts",
    )(g_expert, g_tile0, g_tiles, n_groups, nvalid,
      b_gu.reshape(DEPTH, N_EXPERTS, 1, 2 * D_EXPERT), b_down.reshape(DEPTH, N_EXPERTS, 1, D_MODEL),
      xs, w_gu, w_down)


def _dispatch_plan(idx, rank, cnt):
    counts = cnt[0, :N_EXPERTS].astype(jnp.int32)
    padded = ((counts + MOE_TM - 1) // MOE_TM) * MOE_TM
    gend = jnp.cumsum(padded)
    gstart = gend - padded
    top_i = idx[:, :TOP_K]
    experts = jnp.arange(N_EXPERTS, dtype=jnp.int32)
    start_of = jnp.sum(jnp.where(top_i[:, :, None] == experts, gstart, 0), axis=-1)
    dest = (start_of + rank[:, :TOP_K]).reshape(-1)
    nvalid = gend[-1:] // MOE_TM
    last_tile = gend // MOE_TM - 1
    n_tile = padded // MOE_TM
    n_pass = (n_tile + MOE_S - 1) // MOE_S
    pass_end = jnp.cumsum(n_pass)
    g = jnp.arange(G_MAX, dtype=jnp.int32)
    g_expert = jnp.minimum(jnp.sum((pass_end[None, :] <= g[:, None]).astype(jnp.int32), axis=1),
                           N_EXPERTS - 1)
    onehot = g_expert[:, None] == experts[None, :]

    def of_expert(v):
        return jnp.sum(jnp.where(onehot, v[None, :], 0), axis=1)

    k = g - of_expert(pass_end - n_pass)
    g_tile0 = of_expert(gstart // MOE_TM) + k * MOE_S
    g_tiles = jnp.where(g < pass_end[-1], jnp.clip(of_expert(n_tile) - k * MOE_S, 0, MOE_S), 0)
    groups = (g_expert, g_tile0, g_tiles, pass_end[-1:])
    return dest, groups, nvalid, last_tile, padded


def _row_copies(tile, dest_ref, src_row, dst_row, sem):
    base = tile * (ROW_TILE * TOP_K)

    def body(r, carry):
        for k in range(TOP_K):
            d = dest_ref[base + r * TOP_K + k]
            pltpu.make_async_copy(src_row(r, k, d), dst_row(r, k, d), sem).start(priority=k % 2)
        return carry

    lax.fori_loop(0, ROW_TILE, body, 0, unroll=8)


def _dispatch_kernel(dest_ref, last_tile_ref, padded_ref, nvalid_ref, h_ref, xs_ref, zero_ref,
                     sem, zsem):
    def zero_tile(t):
        rows = pl.ds(pl.multiple_of(t * MOE_TM, MOE_TM), MOE_TM)
        return pltpu.make_async_copy(zero_ref, xs_ref.at[rows], zsem)

    @pl.when(pl.program_id(0) == 0)
    def _():
        zero_ref[...] = jnp.zeros(zero_ref.shape, zero_ref.dtype)
        for do in ("start", "wait"):
            for e in range(N_EXPERTS):
                @pl.when(padded_ref[e] > 0)
                def _():
                    getattr(zero_tile(last_tile_ref[e]), do)()
            for t in range(P_PAIRS // MOE_TM, N_MOE_TILES):
                @pl.when(t >= nvalid_ref[0])
                def _():
                    getattr(zero_tile(t), do)()

    _row_copies(pl.program_id(0), dest_ref,
                lambda r, k, d: h_ref.at[pl.ds(r, 1)],
                lambda r, k, d: xs_ref.at[pl.ds(d, 1)], sem)
    for _ in range(TOP_K):
        pltpu.make_async_copy(h_ref, xs_ref.at[pl.ds(0, ROW_TILE)], sem).wait()


def _dispatch(h, dest, last_tile, padded, nvalid):
    return pl.pallas_call(
        _dispatch_kernel,
        out_shape=jax.ShapeDtypeStruct((P_PAD, D_MODEL), F32),
        grid_spec=pltpu.PrefetchScalarGridSpec(
            num_scalar_prefetch=4,
            grid=(T_ALL // ROW_TILE,),
            in_specs=[pl.BlockSpec((ROW_TILE, D_MODEL), lambda i, *_: (i, 0))],
            out_specs=pl.BlockSpec(memory_space=pl.ANY),
            scratch_shapes=[pltpu.VMEM((MOE_TM, D_MODEL), F32),
                            pltpu.SemaphoreType.DMA(()), pltpu.SemaphoreType.DMA(())],
        ),
        compiler_params=_params("arbitrary"),
        name="dispatch",
    )(dest, last_tile, padded, nvalid, h)


def _combine_kernel(dest_ref, x_ref, wgt_ref, g_ref, lg_ref, lb_ref, sc_ref, sh_ref, y_ref,
                    xo_ref, h_ref, ybuf_ref, sem):
    i = pl.program_id(0)

    def gather(tile, slot):
        _row_copies(tile, dest_ref,
                    lambda r, k, d: y_ref.at[pl.ds(d, 1)],
                    lambda r, k, d: ybuf_ref.at[slot, pl.ds(k * ROW_TILE + r, 1)],
                    sem.at[slot])

    @pl.when(i == 0)
    def _():
        gather(0, 0)

    @pl.when(i + 1 < pl.num_programs(0))
    def _():
        gather(i + 1, (i + 1) % 2)

    slot = i % 2
    pltpu.make_async_copy(y_ref.at[pl.ds(0, TOP_K * ROW_TILE)], ybuf_ref.at[slot],
                          sem.at[slot]).wait()
    wgt = wgt_ref[...]
    moe = jnp.zeros(x_ref.shape, F32)
    for k in range(TOP_K):
        moe = moe + wgt[:, k:k + 1] * ybuf_ref[slot, k * ROW_TILE:(k + 1) * ROW_TILE, :]
    z = DEEP_ALPHA * x_ref[...] + g_ref[...] * moe
    xn = _ln(z) * lg_ref[...] + lb_ref[...]
    xo_ref[...] = xn
    h_ref[...] = (_ln(xn) * (1.0 + sc_ref[...]) + sh_ref[...]).astype(h_ref.dtype)


def _combine_resid_ln(x, y, dest, wgt, mod_gate, gate_idx, ln_g, ln_b, layer, mod_next, sc_idx,
                      sh_idx):
    row = pl.BlockSpec((ROW_TILE, D_MODEL), lambda i, *_: (i, 0))
    vec = pl.BlockSpec((None, 1, D_MODEL), lambda i, *_: (layer, 0, 0))
    return pl.pallas_call(
        _combine_kernel,
        out_shape=(jax.ShapeDtypeStruct((T_ALL, D_MODEL), F32),
                   jax.ShapeDtypeStruct((T_ALL, D_MODEL), BF16)),
        grid_spec=pltpu.PrefetchScalarGridSpec(
            num_scalar_prefetch=1,
            grid=(T_ALL // ROW_TILE,),
            in_specs=[row, pl.BlockSpec((ROW_TILE, LANE), lambda i, *_: (i, 0)),
                      _mod_spec(gate_idx), vec, vec, _mod_spec(sc_idx), _mod_spec(sh_idx),
                      pl.BlockSpec(memory_space=pl.ANY)],
            out_specs=(row, row),
            scratch_shapes=[pltpu.VMEM((2, TOP_K * ROW_TILE, D_MODEL), F32),
                            pltpu.SemaphoreType.DMA((2,))],
        ),
        compiler_params=_params("arbitrary"),
        name="combine_resid_ln",
    )(dest, x, wgt, mod_gate, ln_g.reshape(DEPTH, 1, D_MODEL), ln_b.reshape(DEPTH, 1, D_MODEL),
      mod_next, mod_next, y)


def kernel(x_prompt, x_sample, c, cache_k, cache_v, state_ret, c_ctx, w_mod, b_mod, w_in,
           ret_decay_logit, diff_lambda, conv_w, w_branch, w_mgate, b_mgate, w_out,
           ln1_g, ln1_b, ln2_g, ln2_b, w_router, b_router, w_gu, b_gu, w_down, b_down):
    x = jnp.concatenate([x_prompt.reshape(T_PROMPT, D_MODEL), x_sample.reshape(T_SAMPLE, D_MODEL)], 0)
    cond = jnp.concatenate([c_ctx[None, :], c, jnp.zeros((COND_PAD - N_COND, D_MODEL), F32)], 0)
    ret_tables = _rope_tables(RET_DK)
    diff_tables = _rope_tables(DIFF_DH)
    cw, sw = (t.astype(BF16) for t in _dft_tables(LANE))
    cs_p = jnp.concatenate(_dft_tables(SEQ), axis=1).astype(BF16)
    cs_s = jnp.concatenate(_dft_tables(DEC_SEQ), axis=1).astype(BF16)
    conv_w_t = jnp.swapaxes(conv_w, 1, 2)
    w_router_pad = jnp.pad(w_router, ((0, 0), (0, 0), (0, LANE - N_EXPERTS)))
    b_router_pad = jnp.pad(b_router, ((0, 0), (0, LANE - N_EXPERTS))).reshape(DEPTH, 1, LANE)

    mods = [_modulation(cond, w_mod, b_mod, l).reshape(COND_PAD, 6, 1, D_MODEL)
            for l in range(DEPTH)]
    ks_out, vs_out, ss_out = [], [], []
    h = _ln_mod(x, mods[0], 1, 0)
    for l in range(DEPTH):
        lam_init = 0.8 - 0.6 * math.exp(-0.3 * l)
        proj = _matmul(h, w_in, l, F32, IN_TN, "in_proj")
        proj_p = proj[:T_PROMPT]
        ks_out.append(proj_p[:, COL_DK * LANE:COL_DV * LANE]
                      .reshape(BATCH, SEQ, DIFF_HEADS, 2, DIFF_DH))
        vs_out.append(proj_p[:, COL_DV * LANE:COL_CB * LANE]
                      .reshape(BATCH, SEQ, DIFF_HEADS, 2 * DIFF_DH))
        ret_p, s_fin = _retention(proj, ret_decay_logit, l, use_ctx=False)
        ret_s = _retention(proj, ret_decay_logit, l, use_ctx=True, state_ret=state_ret,
                           tables=ret_tables)
        ss_out.append(s_fin)
        diff_p = _diff_attention(proj, diff_lambda, l, lam_init, use_ctx=False)
        diff_s = _diff_attention(proj, diff_lambda, l, lam_init, use_ctx=True, cache_k=cache_k,
                                 cache_v=cache_v, tables=diff_tables)
        conv_p = _gated_conv(proj, conv_w_t, l, use_ctx=False)
        conv_s = _gated_conv(proj, conv_w_t, l, use_ctx=True)
        four_p = _fourier(proj, cw, sw, cs_p, use_ctx=False)
        four_s = _fourier(proj, cw, sw, cs_s, use_ctx=True)
        merged = _merge(h, [ret_p, diff_p, conv_p, four_p], [ret_s, diff_s, conv_s, four_s],
                        w_mgate, b_mgate, w_branch, l)
        mix = _matmul(merged, w_out, l, F32, MM_TN, "out_proj")
        x, h2 = _resid_ln(x, mix, mods[l], 2, ln1_g, ln1_b, l, mods[l], 4, 3, F32)
        idx, wgt, rank, cnt = _router(h2, w_router_pad, b_router_pad, l)
        dest, groups, nvalid, last_tile, padded = _dispatch_plan(idx, rank, cnt)
        xs = _dispatch(h2, dest, last_tile, padded, nvalid)
        y = _experts(xs, groups, nvalid, w_gu, b_gu, w_down, b_down, l)
        nxt = mods[min(l + 1, DEPTH - 1)]
        x, h = _combine_resid_ln(x, y, dest, wgt, mods[l], 5, ln2_g, ln2_b, l, nxt, 1, 0)
    y_p = x[:T_PROMPT].reshape(BATCH, SEQ, D_MODEL)
    y_s = x[T_PROMPT:].reshape(DEC_BATCH, DEC_SEQ, D_MODEL)
    return (y_p, y_s, jnp.stack(ks_out, axis=1), jnp.stack(vs_out, axis=1),
            jnp.stack(ss_out, axis=1))
```

```python
import functools
import math

import jax
import jax.numpy as jnp
import numpy as np
from jax import lax
from jax.experimental import pallas as pl
from jax.experimental.pallas import tpu as pltpu

D_MODEL = 2048
BATCH = 16
SEQ = 256
DEPTH = 4
DEC_BATCH = 4
DEC_SEQ = 2048
PAST_LEN = 512
GRID_W = 64
N_BRANCH = 4
BRANCH_W = 512
RET_HEADS = 4
RET_DK = 128
RET_CHUNK = 128
DIFF_HEADS = 4
DIFF_DH = 64
N_EXPERTS = 32
TOP_K = 4
D_EXPERT = 2048
SWIGLU_ALPHA = 1.702
SWIGLU_LIMIT = 7.0
ROPE_BASE = 10000.0
LN_EPS = 1e-5
RMS_EPS = 1e-6
DEEP_ALPHA = (2 * DEPTH) ** 0.25
D_IN = 5632

LANE = 128
T_PROMPT = BATCH * SEQ
T_SAMPLE = DEC_BATCH * DEC_SEQ
T_ALL = T_PROMPT + T_SAMPLE
N_COND = 1 + DEC_BATCH
COND_PAD = 8

ROW_TILE = 256
MM_TM = 512
MM_TN = 512
IN_TN = 1408
MERGE_TN = 256
MOE_TM = 256
MOE_S = 8
MOE_FC = 256
N_CHUNK = D_EXPERT // MOE_FC
X_AHEAD = 2
X_SLOTS = X_AHEAD + 1
CAST_ROWS = 256
P_PAIRS = T_ALL * TOP_K
P_PAD = P_PAIRS + N_EXPERTS * MOE_TM
N_MOE_TILES = P_PAD // MOE_TM
G_MAX = N_EXPERTS + N_MOE_TILES // MOE_S
VMEM_LIMIT = 56 * 1024 * 1024

COL_RQ, COL_RK, COL_RV, COL_RG = 0, 4, 8, 12
COL_DQ, COL_DK, COL_DV = 16, 20, 24
COL_CB, COL_CC, COL_CX = 28, 32, 36
COL_FX = 40

BF16 = jnp.bfloat16
F32 = jnp.float32


def _params(*sem):
    return pltpu.CompilerParams(dimension_semantics=sem, vmem_limit_bytes=VMEM_LIMIT)


def _cond_row_of_tile(i, tile):
    start = i * tile
    return jnp.where(start < T_PROMPT, 0, 1 + (start - T_PROMPT) // DEC_SEQ)


def _cache_bf16(src_ref, dst_ref):
    n = src_ref.shape[0] // CAST_ROWS

    def body(i, carry):
        r = pl.multiple_of(i * CAST_ROWS, CAST_ROWS)
        dst_ref[pl.ds(r, CAST_ROWS), :] = src_ref[pl.ds(r, CAST_ROWS), :].astype(BF16)
        return carry

    lax.fori_loop(0, n, body, 0)


def _mod_kernel(c_ref, w_ref, b_ref, o_ref):
    c = c_ref[...]
    a = (c * jax.nn.sigmoid(c)).astype(BF16)
    o_ref[...] = jnp.dot(a, w_ref[...].astype(BF16), preferred_element_type=F32) + b_ref[...]


def _modulation(cond_pad, w_mod, b_mod, layer):
    n = 6 * D_MODEL
    return pl.pallas_call(
        _mod_kernel,
        out_shape=jax.ShapeDtypeStruct((COND_PAD, n), F32),
        grid=(n // MM_TN,),
        in_specs=[
            pl.BlockSpec((COND_PAD, D_MODEL), lambda j: (0, 0)),
            pl.BlockSpec((None, D_MODEL, MM_TN), lambda j: (layer, 0, j)),
            pl.BlockSpec((None, 1, MM_TN), lambda j: (layer, 0, j)),
        ],
        out_specs=pl.BlockSpec((COND_PAD, MM_TN), lambda j: (0, j)),
        compiler_params=_params("arbitrary"),
        name="modulation",
    )(cond_pad, w_mod, b_mod.reshape(DEPTH, 1, n))


def _ln(x):
    xc = x - jnp.mean(x, axis=-1, keepdims=True)
    return xc * lax.rsqrt(jnp.mean(xc * xc, axis=-1, keepdims=True) + LN_EPS)


def _ln_mod_kernel(x_ref, sc_ref, sh_ref, h_ref):
    h_ref[...] = (_ln(x_ref[...]) * (1.0 + sc_ref[...]) + sh_ref[...]).astype(h_ref.dtype)


def _mod_spec(which):
    return pl.BlockSpec((None, None, 1, D_MODEL),
                        lambda i, *_: (_cond_row_of_tile(i, ROW_TILE), which, 0, 0))


def _ln_mod(x, mod4, sc_idx, sh_idx):
    return pl.pallas_call(
        _ln_mod_kernel,
        out_shape=jax.ShapeDtypeStruct((T_ALL, D_MODEL), BF16),
        grid=(T_ALL // ROW_TILE,),
        in_specs=[pl.BlockSpec((ROW_TILE, D_MODEL), lambda i: (i, 0)),
                  _mod_spec(sc_idx), _mod_spec(sh_idx)],
        out_specs=pl.BlockSpec((ROW_TILE, D_MODEL), lambda i: (i, 0)),
        compiler_params=_params("parallel"),
        name="ln_mod",
    )(x, mod4, mod4)


def _resid_ln_kernel(x_ref, y_ref, g_ref, lg_ref, lb_ref, sc_ref, sh_ref, xo_ref, h_ref):
    z = DEEP_ALPHA * x_ref[...] + g_ref[...] * y_ref[...]
    xn = _ln(z) * lg_ref[...] + lb_ref[...]
    xo_ref[...] = xn
    h_ref[...] = (_ln(xn) * (1.0 + sc_ref[...]) + sh_ref[...]).astype(h_ref.dtype)


def _resid_ln(x, y, mod_gate, gate_idx, ln_g, ln_b, layer, mod_next, sc_idx, sh_idx, h_dtype):
    row = pl.BlockSpec((ROW_TILE, D_MODEL), lambda i: (i, 0))
    vec = pl.BlockSpec((None, 1, D_MODEL), lambda i: (layer, 0, 0))
    return pl.pallas_call(
        _resid_ln_kernel,
        out_shape=(jax.ShapeDtypeStruct((T_ALL, D_MODEL), F32),
                   jax.ShapeDtypeStruct((T_ALL, D_MODEL), h_dtype)),
        grid=(T_ALL // ROW_TILE,),
        in_specs=[row, row, _mod_spec(gate_idx), vec, vec, _mod_spec(sc_idx), _mod_spec(sh_idx)],
        out_specs=(row, row),
        compiler_params=_params("parallel"),
        name="resid_ln",
    )(x, y, mod_gate, ln_g.reshape(DEPTH, 1, D_MODEL), ln_b.reshape(DEPTH, 1, D_MODEL),
      mod_next, mod_next)


def _mm_kernel(x_ref, w_ref, o_ref, wb_ref):
    @pl.when(pl.program_id(1) == 0)
    def _():
        _cache_bf16(w_ref, wb_ref)

    o_ref[...] = jnp.dot(x_ref[...], wb_ref[...], preferred_element_type=F32).astype(o_ref.dtype)


def _matmul(x, w, layer, out_dtype, tn, name):
    m, k = x.shape
    n = w.shape[-1]
    return pl.pallas_call(
        _mm_kernel,
        out_shape=jax.ShapeDtypeStruct((m, n), out_dtype),
        grid=(n // tn, m // MM_TM),
        in_specs=[pl.BlockSpec((MM_TM, k), lambda j, i: (i, 0)),
                  pl.BlockSpec((None, k, tn), lambda j, i: (layer, 0, j))],
        out_specs=pl.BlockSpec((MM_TM, tn), lambda j, i: (i, j)),
        scratch_shapes=[pltpu.VMEM((k, tn), BF16)],
        compiler_params=_params("arbitrary", "arbitrary"),
        name=name,
    )(x, w)


def _rope_tables(d):
    rows = DEC_SEQ // GRID_W
    row = jnp.repeat(jnp.arange(rows, dtype=F32), GRID_W)
    col = jnp.tile(jnp.arange(GRID_W, dtype=F32), rows)
    axis_dim = d // 2
    inv_freq = ROPE_BASE ** (-jnp.arange(0, axis_dim, 2, dtype=F32) / axis_dim)
    ang_r = row[:, None] * inv_freq
    ang_c = col[:, None] * inv_freq
    cos = jnp.concatenate([jnp.cos(ang_r), jnp.cos(ang_r), jnp.cos(ang_c), jnp.cos(ang_c)], -1)
    sin = jnp.concatenate([-jnp.sin(ang_r), jnp.sin(ang_r), -jnp.sin(ang_c), jnp.sin(ang_c)], -1)
    reps = LANE // d
    return jnp.tile(cos, (1, reps)), jnp.tile(sin, (1, reps))


def _rope(x, cos, sin, d):
    q = d // 4
    lane = lax.broadcasted_iota(jnp.int32, x.shape, 1)
    first = (lane % (2 * q)) < q
    partner = jnp.where(first, pltpu.roll(x, LANE - q, 1), pltpu.roll(x, q, 1))
    return x * cos + partner * sin


def _retention_kernel(*refs, seq, use_ctx):
    if use_ctx:
        (q_ref, k_ref, v_ref, g_ref, lgt_ref, s0_ref, cos_ref, sin_ref, o_ref,
         of_ref, ob_ref) = refs
    else:
        q_ref, k_ref, v_ref, g_ref, lgt_ref, o_ref, sfin_ref, of_ref, ob_ref = refs
    c = RET_CHUNK
    n_chunk = seq // c
    ii = lax.broadcasted_iota(jnp.int32, (c, c), 0).astype(F32)
    jj = lax.broadcasted_iota(jnp.int32, (c, c), 1).astype(F32)
    col_pos = lax.broadcasted_iota(jnp.int32, (c, 1), 0).astype(F32)
    row_pos = lax.broadcasted_iota(jnp.int32, (1, c), 1).astype(F32)

    def direction(d):
        logit = lgt_ref[d]
        lg = jnp.minimum(logit, 0.0) - jnp.log1p(jnp.exp(-jnp.abs(logit)))
        lg_mat = jnp.broadcast_to(lg[0:1, :], (c, c))
        lg_col = lg[:, 0:1][0:1, :]
        if d == 0:
            dist = ii - jj
            kw = jnp.exp((c - 1 - row_pos) * lg_col)
            qw = jnp.exp((col_pos + 1.0) * lg_col)
        else:
            dist = jj - ii
            kw = jnp.exp(row_pos * lg_col)
            qw = jnp.exp((c - col_pos) * lg_col)
        decay = jnp.where(dist >= 0, jnp.exp(jnp.where(dist >= 0, dist, 0.0) * lg_mat), 0.0)
        chunk_decay = jnp.exp(c * lg_mat)
        return decay, kw, qw, chunk_decay

    def chunk(n):
        rows = slice(n * c, (n + 1) * c)
        q = q_ref[rows, :]
        k = k_ref[rows, :] * (RET_DK ** -0.5)
        if use_ctx:
            q = _rope(q, cos_ref[rows, :], sin_ref[rows, :], RET_DK)
            k = _rope(k, cos_ref[rows, :], sin_ref[rows, :], RET_DK)
        return q, k.T, v_ref[rows, :].astype(BF16)

    for d, acc_ref in ((0, of_ref), (1, ob_ref)):
        decay, kw, qw, chunk_decay = direction(d)
        if use_ctx:
            state = s0_ref[d]
        else:
            state = jnp.zeros((RET_DK, RET_DK), F32)
        order = range(n_chunk) if d == 0 else range(n_chunk - 1, -1, -1)
        for n in order:
            q, kt, v = chunk(n)
            scores = jnp.dot(q.astype(BF16), kt.astype(BF16), preferred_element_type=F32) * decay
            o = jnp.dot(scores.astype(BF16), v, preferred_element_type=F32)
            o = o + jnp.dot((q * qw).astype(BF16), state.astype(BF16), preferred_element_type=F32)
            acc_ref[n * c:(n + 1) * c, :] = o
            state = chunk_decay * state + jnp.dot((kt * kw).astype(BF16), v,
                                                  preferred_element_type=F32)
        if not use_ctx:
            sfin_ref[d] = state

    o = of_ref[...] + ob_ref[...]
    o = o * lax.rsqrt(jnp.mean(o * o, axis=-1, keepdims=True) + RMS_EPS)
    g = g_ref[...]
    o_ref[...] = (g * jax.nn.sigmoid(g) * o).astype(o_ref.dtype)


def _retention(proj, decay_logit, layer, *, use_ctx, state_ret=None, tables=None):
    seq, nb, row0 = (DEC_SEQ, DEC_BATCH, T_PROMPT // DEC_SEQ) if use_ctx else (SEQ, BATCH, 0)

    def col(c0):
        return pl.BlockSpec((seq, LANE), lambda b, h: (row0 + b, c0 + h))

    lgt = jnp.broadcast_to(decay_logit[layer][:, :, None, None], (2, RET_HEADS, 8, LANE))
    in_specs = [col(COL_RQ), col(COL_RK), col(COL_RV), col(COL_RG),
                pl.BlockSpec((2, None, 8, LANE), lambda b, h: (0, h, 0, 0))]
    args = [proj, proj, proj, proj, lgt]
    out_branch = jax.ShapeDtypeStruct((nb * seq, BRANCH_W), BF16)
    branch_spec = pl.BlockSpec((seq, LANE), lambda b, h: (b, h))
    if use_ctx:
        in_specs += [pl.BlockSpec((None, None, 2, None, RET_DK, RET_DK),
                                  lambda b, h: (b, layer, 0, h, 0, 0)),
                     pl.BlockSpec((seq, LANE), lambda b, h: (0, 0)),
                     pl.BlockSpec((seq, LANE), lambda b, h: (0, 0))]
        args += [state_ret, tables[0], tables[1]]
        out_shape, out_specs = out_branch, branch_spec
    else:
        out_shape = (out_branch,
                     jax.ShapeDtypeStruct((nb, 2, RET_HEADS, RET_DK, RET_DK), F32))
        out_specs = (branch_spec,
                     pl.BlockSpec((None, 2, None, RET_DK, RET_DK), lambda b, h: (b, 0, h, 0, 0)))
    return pl.pallas_call(
        functools.partial(_retention_kernel, seq=seq, use_ctx=use_ctx),
        out_shape=out_shape,
        grid=(nb, RET_HEADS),
        in_specs=in_specs,
        out_specs=out_specs,
        scratch_shapes=[pltpu.VMEM((seq, LANE), F32), pltpu.VMEM((seq, LANE), F32)],
        compiler_params=_params("parallel", "parallel"),
        name="retention_ctx" if use_ctx else "retention",
    )(*args)


def _diff_attn_kernel(*refs, seq, tq, use_ctx, lam_init):
    if use_ctx:
        (q_ref, k_ref, v_ref, lam_ref, kc_ref, vc_ref, cosq_ref, sinq_ref, cosk_ref, sink_ref,
         o_ref, kt_ref, va_ref) = refs
    else:
        q_ref, k_ref, v_ref, lam_ref, o_ref, kt_ref, va_ref = refs
    c = LANE

    @pl.when(pl.program_id(2) == 0)
    def _():
        for n in range(seq // c):
            rows = slice(n * c, (n + 1) * c)
            k = k_ref[rows, :]
            if use_ctx:
                k = _rope(k, cosk_ref[rows, :], sink_ref[rows, :], DIFF_DH)
            kt_ref[:, rows] = k.T.astype(BF16)
        va_ref[0:seq, :] = v_ref[...].astype(BF16)
        if use_ctx:
            for n in range(PAST_LEN // c):
                rows = slice(n * c, (n + 1) * c)
                kt_ref[:, seq + n * c:seq + (n + 1) * c] = kc_ref[rows, :].T.astype(BF16)
            va_ref[seq:seq + PAST_LEN, :] = vc_ref[...].astype(BF16)

    lam_v = lam_ref[...]
    lam = (jnp.exp(jnp.sum(lam_v[0:1] * lam_v[1:2], axis=-1, keepdims=True))
           - jnp.exp(jnp.sum(lam_v[2:3] * lam_v[3:4], axis=-1, keepdims=True)) + lam_init)
    q = q_ref[...]
    if use_ctx:
        q = _rope(q, cosq_ref[...], sinq_ref[...], DIFF_DH)
    lane = lax.broadcasted_iota(jnp.int32, q.shape, 1)
    q = q * (DIFF_DH ** -0.5)
    kt = kt_ref[...]
    probs = []
    for m in range(2):
        qm = jnp.where((lane < DIFF_DH) == (m == 0), q, 0.0).astype(BF16)
        s = jnp.dot(qm, kt, preferred_element_type=F32)
        p = jnp.exp(s - jnp.max(s, axis=-1, keepdims=True))
        inv = 1.0 / jnp.sum(p, axis=-1, keepdims=True)
        probs.append((p, inv))
    w = probs[0][0] * probs[0][1] - probs[1][0] * (lam * probs[1][1])
    o = jnp.dot(w.astype(BF16), va_ref[...], preferred_element_type=F32)
    o = o * lax.rsqrt(jnp.mean(o * o, axis=-1, keepdims=True) + RMS_EPS) * (1.0 - lam_init)
    o_ref[...] = o.astype(o_ref.dtype)


def _diff_attention(proj, diff_lambda, layer, lam_init, *, use_ctx, cache_k=None, cache_v=None,
                    tables=None):
    seq, nb, row0 = (DEC_SEQ, DEC_BATCH, T_PROMPT // DEC_SEQ) if use_ctx else (SEQ, BATCH, 0)
    tq = 256
    n_q = seq // tq
    n_keys = seq + (PAST_LEN if use_ctx else 0)
    in_specs = [pl.BlockSpec((tq, LANE), lambda b, h, i: ((row0 + b) * n_q + i, COL_DQ + h)),
                pl.BlockSpec((seq, LANE), lambda b, h, i: (row0 + b, COL_DK + h)),
                pl.BlockSpec((seq, LANE), lambda b, h, i: (row0 + b, COL_DV + h)),
                pl.BlockSpec((None, 4, DIFF_DH), lambda b, h, i: (layer, 0, 0))]
    args = [proj, proj, proj, diff_lambda]
    if use_ctx:
        ctx_spec = pl.BlockSpec((None, None, PAST_LEN, LANE), lambda b, h, i: (b, layer, 0, h))
        full = pl.BlockSpec((seq, LANE), lambda b, h, i: (0, 0))
        qtab = pl.BlockSpec((tq, LANE), lambda b, h, i: (i, 0))
        in_specs += [ctx_spec, ctx_spec, qtab, qtab, full, full]
        args += [cache_k.reshape(DEC_BATCH, DEPTH, PAST_LEN, DIFF_HEADS * 2 * DIFF_DH),
                 cache_v.reshape(DEC_BATCH, DEPTH, PAST_LEN, DIFF_HEADS * 2 * DIFF_DH),
                 tables[0], tables[1], tables[0], tables[1]]
    return pl.pallas_call(
        functools.partial(_diff_attn_kernel, seq=seq, tq=tq, use_ctx=use_ctx, lam_init=lam_init),
        out_shape=jax.ShapeDtypeStruct((nb * seq, BRANCH_W), BF16),
        grid=(nb, DIFF_HEADS, n_q),
        in_specs=in_specs,
        out_specs=pl.BlockSpec((tq, LANE), lambda b, h, i: (b * n_q + i, h)),
        scratch_shapes=[pltpu.VMEM((LANE, n_keys), BF16), pltpu.VMEM((n_keys, LANE), BF16)],
        compiler_params=_params("parallel", "parallel", "arbitrary"),
        name="diff_attention_ctx" if use_ctx else "diff_attention",
    )(*args)


def _conv_kernel(cb_ref, cc_ref, cx_ref, w_ref, o_ref, *, seq):
    u = cc_ref[...] * cx_ref[...]
    row = lax.broadcasted_iota(jnp.int32, u.shape, 0)
    prev = jnp.where(row == 0, 0.0, pltpu.roll(u, 1, 0))
    nxt = jnp.where(row == seq - 1, 0.0, pltpu.roll(u, seq - 1, 0))
    w = w_ref[...]
    conv = prev * w[0:1, :] + u * w[1:2, :] + nxt * w[2:3, :]
    o_ref[...] = (cb_ref[...] * conv).astype(o_ref.dtype)


def _gated_conv(proj, conv_w_t, layer, *, use_ctx):
    seq, nb, row0 = (DEC_SEQ, DEC_BATCH, T_PROMPT // DEC_SEQ) if use_ctx else (SEQ, BATCH, 0)
    n_c = BRANCH_W // LANE

    def col(c0):
        return pl.BlockSpec((seq, LANE), lambda b, c: (row0 + b, c0 + c))

    return pl.pallas_call(
        functools.partial(_conv_kernel, seq=seq),
        out_shape=jax.ShapeDtypeStruct((nb * seq, BRANCH_W), BF16),
        grid=(nb, n_c),
        in_specs=[col(COL_CB), col(COL_CC), col(COL_CX),
                  pl.BlockSpec((None, 3, LANE), lambda b, c: (layer, 0, c))],
        out_specs=pl.BlockSpec((seq, LANE), lambda b, c: (b, c)),
        compiler_params=_params("parallel", "parallel"),
        name="gated_conv_ctx" if use_ctx else "gated_conv",
    )(proj, proj, proj, conv_w_t)


def _dft_tables(n):
    j = jnp.arange(n, dtype=jnp.int32)
    m = (j[:, None] * j[None, :]) % n
    ang = m.astype(F32) * (2.0 * math.pi / n)
    return jnp.cos(ang), jnp.sin(ang)


def _fourier_kernel(x_ref, cw_ref, sw_ref, cs_ref, o_ref, y_ref, *, seq):
    @pl.when(pl.program_id(1) == 0)
    def _():
        for g in range(BRANCH_W // LANE):
            cols = slice(g * LANE, (g + 1) * LANE)
            xg = x_ref[:, cols].astype(BF16)
            y_ref[0:seq, cols] = jnp.dot(xg, cw_ref[...], preferred_element_type=F32).astype(BF16)
            y_ref[seq:2 * seq, cols] = (-jnp.dot(xg, sw_ref[...],
                                                 preferred_element_type=F32)).astype(BF16)

    o = jnp.dot(cs_ref[...], y_ref[...], preferred_element_type=F32)
    o_ref[...] = (o * ((seq * LANE) ** -0.5)).astype(o_ref.dtype)


def _fourier(proj, cw, sw, cs_l, *, use_ctx):
    seq, nb, row0 = (DEC_SEQ, DEC_BATCH, T_PROMPT // DEC_SEQ) if use_ctx else (SEQ, BATCH, 0)
    tl = 256
    n_t = seq // tl
    wide = BRANCH_W // LANE
    return pl.pallas_call(
        functools.partial(_fourier_kernel, seq=seq),
        out_shape=jax.ShapeDtypeStruct((nb * seq, BRANCH_W), BF16),
        grid=(nb, n_t),
        in_specs=[pl.BlockSpec((seq, BRANCH_W), lambda b, i: (row0 + b, COL_FX // wide)),
                  pl.BlockSpec((LANE, LANE), lambda b, i: (0, 0)),
                  pl.BlockSpec((LANE, LANE), lambda b, i: (0, 0)),
                  pl.BlockSpec((tl, 2 * seq), lambda b, i: (i, 0))],
        out_specs=pl.BlockSpec((tl, BRANCH_W), lambda b, i: (b * n_t + i, 0)),
        scratch_shapes=[pltpu.VMEM((2 * seq, BRANCH_W), BF16)],
        compiler_params=_params("parallel", "arbitrary"),
        name="fourier_ctx" if use_ctx else "fourier",
    )(proj, cw, sw, cs_l)


def _merge_kernel(h_ref, *refs):
    ctx_refs = refs[0:N_BRANCH]
    lat_refs = refs[N_BRANCH:2 * N_BRANCH]
    wg_refs = refs[2 * N_BRANCH:3 * N_BRANCH]
    bg_refs = refs[3 * N_BRANCH:4 * N_BRANCH]
    wb_ref, o_ref, wgb_ref, wbb_ref = refs[4 * N_BRANCH:]
    is_ctx = pl.program_id(1) < T_PROMPT // MM_TM

    @pl.when(pl.program_id(1) == 0)
    def _():
        for n in range(N_BRANCH):
            _cache_bf16(wg_refs[n], wgb_ref.at[n])
            wbb_ref[n] = wb_ref[n].astype(BF16)

    h = h_ref[...]
    acc = jnp.zeros(o_ref.shape, F32)
    for n in range(N_BRANCH):
        gate = jax.nn.sigmoid(jnp.dot(h, wgb_ref[n], preferred_element_type=F32) + bg_refs[n][...])
        branch = jnp.where(is_ctx, ctx_refs[n][...], lat_refs[n][...])
        acc = acc + gate * jnp.dot(branch, wbb_ref[n], preferred_element_type=F32)
    o_ref[...] = acc.astype(o_ref.dtype)


def _merge(h, ctx_branches, lat_branches, w_mgate, b_mgate, w_branch, layer):
    tm, tn = MM_TM, MERGE_TN
    n_t = D_MODEL // tn
    n_ctx = T_PROMPT // tm
    ctx_spec = pl.BlockSpec((tm, BRANCH_W), lambda j, i: (jnp.minimum(i, n_ctx - 1), 0))
    lat_spec = pl.BlockSpec((tm, BRANCH_W), lambda j, i: (jnp.maximum(i - n_ctx, 0), 0))

    def gate_w(n):
        return pl.BlockSpec((None, D_MODEL, tn), lambda j, i: (layer, 0, n * n_t + j))

    def gate_b(n):
        return pl.BlockSpec((None, 1, tn), lambda j, i: (layer, 0, n * n_t + j))

    return pl.pallas_call(
        _merge_kernel,
        out_shape=jax.ShapeDtypeStruct((T_ALL, D_MODEL), BF16),
        grid=(n_t, T_ALL // tm),
        in_specs=[pl.BlockSpec((tm, D_MODEL), lambda j, i: (i, 0))]
        + [ctx_spec] * N_BRANCH + [lat_spec] * N_BRANCH
        + [gate_w(n) for n in range(N_BRANCH)]
        + [gate_b(n) for n in range(N_BRANCH)]
        + [pl.BlockSpec((None, N_BRANCH, BRANCH_W, tn), lambda j, i: (layer, 0, 0, j))],
        out_specs=pl.BlockSpec((tm, tn), lambda j, i: (i, j)),
        scratch_shapes=[pltpu.VMEM((N_BRANCH, D_MODEL, tn), BF16),
                        pltpu.VMEM((N_BRANCH, BRANCH_W, tn), BF16)],
        compiler_params=_params("arbitrary", "arbitrary"),
        name="merge",
    )(h, *ctx_branches, *lat_branches, *([w_mgate] * N_BRANCH),
      *([b_mgate.reshape(DEPTH, 1, N_BRANCH * D_MODEL)] * N_BRANCH), w_branch)


def _router_kernel(h_ref, w_ref, b_ref, idx_ref, wgt_ref, rank_ref, cnt_ref, run_ref):
    @pl.when(pl.program_id(0) == 0)
    def _():
        run_ref[...] = jnp.zeros(run_ref.shape, F32)

    logits = jnp.dot(h_ref[...].astype(BF16), w_ref[...].astype(BF16),
                     preferred_element_type=F32) + b_ref[...]
    lane = lax.broadcasted_iota(jnp.int32, logits.shape, 1).astype(F32)
    neg = jnp.float32(-jnp.inf)
    logits = jnp.where(lane < N_EXPERTS, logits, neg)
    idx_out = jnp.zeros(logits.shape, F32)
    val_out = jnp.zeros(logits.shape, F32)
    chosen = jnp.zeros(logits.shape, F32)
    picks = []
    top = None
    denom = None
    for k in range(TOP_K):
        m = jnp.max(logits, axis=-1, keepdims=True)
        idx = jnp.min(jnp.where(logits == m, lane, float(LANE)), axis=-1, keepdims=True)
        if k == 0:
            top = m
        e = jnp.exp(m - top)
        denom = e if k == 0 else denom + e
        idx_out = jnp.where(lane == k, idx, idx_out)
        val_out = jnp.where(lane == k, e, val_out)
        hit = lane == idx
        chosen = jnp.where(hit, 1.0, chosen)
        logits = jnp.where(hit, neg, logits)
        picks.append(hit)
    idx_ref[...] = idx_out.astype(jnp.int32)
    wgt_ref[...] = val_out / denom

    tm = logits.shape[0]
    earlier = (lax.broadcasted_iota(jnp.int32, (tm, tm), 1)
               < lax.broadcasted_iota(jnp.int32, (tm, tm), 0))
    before = jnp.dot(jnp.where(earlier, 1.0, 0.0).astype(BF16), chosen.astype(BF16),
                     preferred_element_type=F32) + run_ref[0:1, :]
    rank_out = jnp.zeros(logits.shape, F32)
    for k in range(TOP_K):
        r = jnp.sum(jnp.where(picks[k], before, 0.0), axis=-1, keepdims=True)
        rank_out = jnp.where(lane == k, r, rank_out)
    rank_ref[...] = rank_out.astype(jnp.int32)
    total = run_ref[0:1, :] + jnp.sum(chosen, axis=0, keepdims=True)
    run_ref[...] = jnp.broadcast_to(total, run_ref.shape)
    cnt_ref[...] = jnp.broadcast_to(total, cnt_ref.shape)


def _router(h, w_router_pad, b_router_pad, layer):
    row = pl.BlockSpec((ROW_TILE, LANE), lambda i: (i, 0))
    return pl.pallas_call(
        _router_kernel,
        out_shape=(jax.ShapeDtypeStruct((T_ALL, LANE), jnp.int32),
                   jax.ShapeDtypeStruct((T_ALL, LANE), F32),
                   jax.ShapeDtypeStruct((T_ALL, LANE), jnp.int32),
                   jax.ShapeDtypeStruct((8, LANE), F32)),
        grid=(T_ALL // ROW_TILE,),
        in_specs=[pl.BlockSpec((ROW_TILE, D_MODEL), lambda i: (i, 0)),
                  pl.BlockSpec((None, D_MODEL, LANE), lambda i: (layer, 0, 0)),
                  pl.BlockSpec((None, 1, LANE), lambda i: (layer, 0, 0))],
        out_specs=(row, row, row, pl.BlockSpec((8, LANE), lambda i: (0, 0))),
        scratch_shapes=[pltpu.VMEM((8, LANE), F32)],
        compiler_params=_params("arbitrary"),
        name="router",
    )(h, w_router_pad, b_router_pad)


def _expert_kernel(ge_ref, gt0_ref, gnt_ref, ng_ref, nvalid_ref, bgu_ref, bdn_ref, xs_ref, wgu_ref,
                   wdn_ref, y_ref, xland_ref, xbf_ref, acc_ref, wg_ref, wl_ref, wd_ref, wgb_ref,
                   wlb_ref, wdb_ref, out_ref, xsem, wsem, osem, zsem, *, layer):
    g = pl.program_id(0)
    e, t0, nt, ng = ge_ref[g], gt0_ref[g], gnt_ref[g], ng_ref[0]
    fc = MOE_FC

    def weight_copies(expert, c, slot):
        return (pltpu.make_async_copy(wgu_ref.at[layer, expert, :, pl.ds(c * fc, fc)],
                                      wg_ref.at[slot], wsem.at[slot]),
                pltpu.make_async_copy(wgu_ref.at[layer, expert, :, pl.ds(D_EXPERT + c * fc, fc)],
                                      wl_ref.at[slot], wsem.at[slot]),
                pltpu.make_async_copy(wdn_ref.at[layer, expert, pl.ds(c * fc, fc), :],
                                      wd_ref.at[slot], wsem.at[slot]))

    def tile_rows(tile):
        return pl.ds(pl.multiple_of(tile * MOE_TM, MOE_TM), MOE_TM)

    def x_copy(tile, slot):
        return pltpu.make_async_copy(xs_ref.at[tile_rows(tile)], xland_ref.at[slot], xsem.at[slot])

    def y_copy(tile, slot):
        return pltpu.make_async_copy(out_ref.at[slot], y_ref.at[tile_rows(tile)], osem.at[slot])

    def start_first_x(first_tile, n_tiles):
        for j in range(X_AHEAD):
            @pl.when(j < n_tiles)
            def _():
                x_copy(first_tile + j, j).start()

    @pl.when(g == 0)
    def _():
        out_ref[0] = jnp.zeros(out_ref.shape[1:], out_ref.dtype)
        for do in ("start", "wait"):
            for t in range(P_PAIRS // MOE_TM, N_MOE_TILES):
                @pl.when(t >= nvalid_ref[0])
                def _():
                    getattr(pltpu.make_async_copy(out_ref.at[0], y_ref.at[pl.ds(t * MOE_TM, MOE_TM)],
                                                  zsem), do)()
        for cp in weight_copies(e, 0, 0):
            cp.start()
        start_first_x(t0, nt)

    @pl.when(g < ng)
    def _():
        n_pair, odd = nt // 2, nt % 2
        for c in range(N_CHUNK):
            slot = c % 2
            last = c + 1 == N_CHUNK
            for cp in weight_copies(e, c, slot):
                cp.wait()
            if not last:
                for cp in weight_copies(e, c + 1, 1 - slot):
                    cp.start()
            else:
                @pl.when(g + 1 < ng)
                def _():
                    for cp in weight_copies(ge_ref[g + 1], 0, 0):
                        cp.start()
                    start_first_x(gt0_ref[g + 1], gnt_ref[g + 1])
            _cache_bf16(wg_ref.at[slot], wgb_ref)
            _cache_bf16(wl_ref.at[slot], wlb_ref)
            wdb_ref[...] = wd_ref[slot].astype(BF16)
            bg = bgu_ref[:, c * fc:(c + 1) * fc]
            bl = bgu_ref[:, D_EXPERT + c * fc:D_EXPERT + (c + 1) * fc]

            def tiles(m0, n_tile, have_pending, c=c, last=last, bg=bg, bl=bl):
                if c == 0:
                    for j in range(n_tile):
                        m = m0 + j
                        xslot = m % X_SLOTS
                        x_copy(t0 + m, xslot).wait()

                        @pl.when(m + X_AHEAD < nt)
                        def _():
                            x_copy(t0 + m + X_AHEAD, (m + X_AHEAD) % X_SLOTS).start()

                        xbf_ref[pl.ds(pl.multiple_of(m * MOE_TM, MOE_TM), MOE_TM), :] = (
                            xland_ref[xslot].astype(BF16))
                rows = pl.ds(pl.multiple_of(m0 * MOE_TM, MOE_TM), n_tile * MOE_TM)
                x = xbf_ref[rows, :]
                gate = jnp.dot(x, wgb_ref[...], preferred_element_type=F32) + bg
                lin = jnp.dot(x, wlb_ref[...], preferred_element_type=F32) + bl
                gate = jnp.minimum(gate, SWIGLU_LIMIT)
                lin = jnp.clip(lin, -SWIGLU_LIMIT, SWIGLU_LIMIT)
                act = ((lin + 1.0) * gate * jax.nn.sigmoid(SWIGLU_ALPHA * gate)).astype(BF16)
                part = jnp.dot(act, wdb_ref[...], preferred_element_type=F32)
                if c == 0:
                    acc_ref[rows, :] = part + bdn_ref[...]
                elif not last:
                    acc_ref[rows, :] += part
                else:
                    done = acc_ref[rows, :] + part
                    for j in range(n_tile):
                        @pl.when(have_pending)
                        def _():
                            y_copy(t0, j).wait()

                        out_ref[j] = done[j * MOE_TM:(j + 1) * MOE_TM]
                        y_copy(t0 + m0 + j, j).start()

            def pair_body(p, carry):
                tiles(2 * p, 2, p >= 1)
                return carry

            lax.fori_loop(0, n_pair, pair_body, 0)

            @pl.when(odd == 1)
            def _():
                tiles(nt - 1, 1, n_pair >= 1)

        y_copy(t0, 0).wait()

        @pl.when(n_pair >= 1)
        def _():
            y_copy(t0, 1).wait()


def _experts(xs, groups, nvalid, w_gu, b_gu, w_down, b_down, layer):
    g_expert, g_tile0, g_tiles, n_groups = groups
    any_spec = pl.BlockSpec(memory_space=pl.ANY)
    return pl.pallas_call(
        functools.partial(_expert_kernel, layer=layer),
        out_shape=jax.ShapeDtypeStruct((P_PAD, D_MODEL), F32),
        grid_spec=pltpu.PrefetchScalarGridSpec(
            num_scalar_prefetch=5,
            grid=(G_MAX,),
            in_specs=[
                pl.BlockSpec((None, None, 1, 2 * D_EXPERT), lambda g, ge, *_: (layer, ge[g], 0, 0)),
                pl.BlockSpec((None, None, 1, D_MODEL), lambda g, ge, *_: (layer, ge[g], 0, 0)),
                any_spec, any_spec, any_spec,
            ],
            out_specs=any_spec,
            scratch_shapes=[
                pltpu.VMEM((X_SLOTS, MOE_TM, D_MODEL), F32),
                pltpu.VMEM((MOE_S * MOE_TM, D_MODEL), BF16),
                pltpu.VMEM((MOE_S * MOE_TM, D_MODEL), F32),
                pltpu.VMEM((2, D_MODEL, MOE_FC), F32),
                pltpu.VMEM((2, D_MODEL, MOE_FC), F32),
                pltpu.VMEM((2, MOE_FC, D_MODEL), F32),
                pltpu.VMEM((D_MODEL, MOE_FC), BF16),
                pltpu.VMEM((D_MODEL, MOE_FC), BF16),
                pltpu.VMEM((MOE_FC, D_MODEL), BF16),
                pltpu.VMEM((2, MOE_TM, D_MODEL), F32),
                pltpu.SemaphoreType.DMA((X_SLOTS,)), pltpu.SemaphoreType.DMA((2,)),
                pltpu.SemaphoreType.DMA((2,)), pltpu.SemaphoreType.DMA(()),
            ],
        ),
        compiler_params=_params("arbitrary"),
        name="experts",
    )(g_expert, g_tile0, g_tiles, n_groups, nvalid,
      b_gu.reshape(DEPTH, N_EXPERTS, 1, 2 * D_EXPERT), b_down.reshape(DEPTH, N_EXPERTS, 1, D_MODEL),
      xs, w_gu, w_down)


def _dispatch_plan(idx, rank, cnt):
    counts = cnt[0, :N_EXPERTS].astype(jnp.int32)
    padded = ((counts + MOE_TM - 1) // MOE_TM) * MOE_TM
    gend = jnp.cumsum(padded)
    gstart = gend - padded
    top_i = idx[:, :TOP_K]
    experts = jnp.arange(N_EXPERTS, dtype=jnp.int32)
    start_of = jnp.sum(jnp.where(top_i[:, :, None] == experts, gstart, 0), axis=-1)
    dest = (start_of + rank[:, :TOP_K]).reshape(-1)
    nvalid = gend[-1:] // MOE_TM
    last_tile = gend // MOE_TM - 1
    n_tile = padded // MOE_TM
    n_pass = (n_tile + MOE_S - 1) // MOE_S
    pass_end = jnp.cumsum(n_pass)
    g = jnp.arange(G_MAX, dtype=jnp.int32)
    g_expert = jnp.minimum(jnp.sum((pass_end[None, :] <= g[:, None]).astype(jnp.int32), axis=1),
                           N_EXPERTS - 1)
    onehot = g_expert[:, None] == experts[None, :]

    def of_expert(v):
        return jnp.sum(jnp.where(onehot, v[None, :], 0), axis=1)

    k = g - of_expert(pass_end - n_pass)
    g_tile0 = of_expert(gstart // MOE_TM) + k * MOE_S
    g_tiles = jnp.where(g < pass_end[-1], jnp.clip(of_expert(n_tile) - k * MOE_S, 0, MOE_S), 0)
    groups = (g_expert, g_tile0, g_tiles, pass_end[-1:])
    return dest, groups, nvalid, last_tile, padded


def _row_copies(tile, dest_ref, src_row, dst_row, sem):
    base = tile * (ROW_TILE * TOP_K)

    def body(r, carry):
        for k in range(TOP_K):
            d = dest_ref[base + r * TOP_K + k]
            pltpu.make_async_copy(src_row(r, k, d), dst_row(r, k, d), sem).start(priority=k % 2)
        return carry

    lax.fori_loop(0, ROW_TILE, body, 0, unroll=8)


def _dispatch_kernel(dest_ref, last_tile_ref, padded_ref, nvalid_ref, h_ref, xs_ref, zero_ref,
                     sem, zsem):
    def zero_tile(t):
        rows = pl.ds(pl.multiple_of(t * MOE_TM, MOE_TM), MOE_TM)
        return pltpu.make_async_copy(zero_ref, xs_ref.at[rows], zsem)

    @pl.when(pl.program_id(0) == 0)
    def _():
        zero_ref[...] = jnp.zeros(zero_ref.shape, zero_ref.dtype)
        for do in ("start", "wait"):
            for e in range(N_EXPERTS):
                @pl.when(padded_ref[e] > 0)
                def _():
                    getattr(zero_tile(last_tile_ref[e]), do)()
            for t in range(P_PAIRS // MOE_TM, N_MOE_TILES):
                @pl.when(t >= nvalid_ref[0])
                def _():
                    getattr(zero_tile(t), do)()

    _row_copies(pl.program_id(0), dest_ref,
                lambda r, k, d: h_ref.at[pl.ds(r, 1)],
                lambda r, k, d: xs_ref.at[pl.ds(d, 1)], sem)
    for _ in range(TOP_K):
        pltpu.make_async_copy(h_ref, xs_ref.at[pl.ds(0, ROW_TILE)], sem).wait()


def _dispatch(h, dest, last_tile, padded, nvalid):
    return pl.pallas_call(
        _dispatch_kernel,
        out_shape=jax.ShapeDtypeStruct((P_PAD, D_MODEL), F32),
        grid_spec=pltpu.PrefetchScalarGridSpec(
            num_scalar_prefetch=4,
            grid=(T_ALL // ROW_TILE,),
            in_specs=[pl.BlockSpec((ROW_TILE, D_MODEL), lambda i, *_: (i, 0))],
            out_specs=pl.BlockSpec(memory_space=pl.ANY),
            scratch_shapes=[pltpu.VMEM((MOE_TM, D_MODEL), F32),
                            pltpu.SemaphoreType.DMA(()), pltpu.SemaphoreType.DMA(())],
        ),
        compiler_params=_params("arbitrary"),
        name="dispatch",
    )(dest, last_tile, padded, nvalid, h)


def _combine_kernel(dest_ref, x_ref, wgt_ref, g_ref, lg_ref, lb_ref, sc_ref, sh_ref, y_ref,
                    xo_ref, h_ref, ybuf_ref, sem):
    i = pl.program_id(0)

    def gather(tile, slot):
        _row_copies(tile, dest_ref,
                    lambda r, k, d: y_ref.at[pl.ds(d, 1)],
                    lambda r, k, d: ybuf_ref.at[slot, pl.ds(k * ROW_TILE + r, 1)],
                    sem.at[slot])

    @pl.when(i == 0)
    def _():
        gather(0, 0)

    @pl.when(i + 1 < pl.num_programs(0))
    def _():
        gather(i + 1, (i + 1) % 2)

    slot = i % 2
    pltpu.make_async_copy(y_ref.at[pl.ds(0, TOP_K * ROW_TILE)], ybuf_ref.at[slot],
                          sem.at[slot]).wait()
    wgt = wgt_ref[...]
    moe = jnp.zeros(x_ref.shape, F32)
    for k in range(TOP_K):
        moe = moe + wgt[:, k:k + 1] * ybuf_ref[slot, k * ROW_TILE:(k + 1) * ROW_TILE, :]
    z = DEEP_ALPHA * x_ref[...] + g_ref[...] * moe
    xn = _ln(z) * lg_ref[...] + lb_ref[...]
    xo_ref[...] = xn
    h_ref[...] = (_ln(xn) * (1.0 + sc_ref[...]) + sh_ref[...]).astype(h_ref.dtype)


def _combine_resid_ln(x, y, dest, wgt, mod_gate, gate_idx, ln_g, ln_b, layer, mod_next, sc_idx,
                      sh_idx):
    row = pl.BlockSpec((ROW_TILE, D_MODEL), lambda i, *_: (i, 0))
    vec = pl.BlockSpec((None, 1, D_MODEL), lambda i, *_: (layer, 0, 0))
    return pl.pallas_call(
        _combine_kernel,
        out_shape=(jax.ShapeDtypeStruct((T_ALL, D_MODEL), F32),
                   jax.ShapeDtypeStruct((T_ALL, D_MODEL), BF16)),
        grid_spec=pltpu.PrefetchScalarGridSpec(
            num_scalar_prefetch=1,
            grid=(T_ALL // ROW_TILE,),
            in_specs=[row, pl.BlockSpec((ROW_TILE, LANE), lambda i, *_: (i, 0)),
                      _mod_spec(gate_idx), vec, vec, _mod_spec(sc_idx), _mod_spec(sh_idx),
                      pl.BlockSpec(memory_space=pl.ANY)],
            out_specs=(row, row),
            scratch_shapes=[pltpu.VMEM((2, TOP_K * ROW_TILE, D_MODEL), F32),
                            pltpu.SemaphoreType.DMA((2,))],
        ),
        compiler_params=_params("arbitrary"),
        name="combine_resid_ln",
    )(dest, x, wgt, mod_gate, ln_g.reshape(DEPTH, 1, D_MODEL), ln_b.reshape(DEPTH, 1, D_MODEL),
      mod_next, mod_next, y)


def kernel(x_prompt, x_sample, c, cache_k, cache_v, state_ret, c_ctx, w_mod, b_mod, w_in,
           ret_decay_logit, diff_lambda, conv_w, w_branch, w_mgate, b_mgate, w_out,
           ln1_g, ln1_b, ln2_g, ln2_b, w_router, b_router, w_gu, b_gu, w_down, b_down):
    x = jnp.concatenate([x_prompt.reshape(T_PROMPT, D_MODEL), x_sample.reshape(T_SAMPLE, D_MODEL)], 0)
    cond = jnp.concatenate([c_ctx[None, :], c, jnp.zeros((COND_PAD - N_COND, D_MODEL), F32)], 0)
    ret_tables = _rope_tables(RET_DK)
    diff_tables = _rope_tables(DIFF_DH)
    cw, sw = (t.astype(BF16) for t in _dft_tables(LANE))
    cs_p = jnp.concatenate(_dft_tables(SEQ), axis=1).astype(BF16)
    cs_s = jnp.concatenate(_dft_tables(DEC_SEQ), axis=1).astype(BF16)
    conv_w_t = jnp.swapaxes(conv_w, 1, 2)
    w_router_pad = jnp.pad(w_router, ((0, 0), (0, 0), (0, LANE - N_EXPERTS)))
    b_router_pad = jnp.pad(b_router, ((0, 0), (0, LANE - N_EXPERTS))).reshape(DEPTH, 1, LANE)

    mods = [_modulation(cond, w_mod, b_mod, l).reshape(COND_PAD, 6, 1, D_MODEL)
            for l in range(DEPTH)]
    ks_out, vs_out, ss_out = [], [], []
    h = _ln_mod(x, mods[0], 1, 0)
    for l in range(DEPTH):
        lam_init = 0.8 - 0.6 * math.exp(-0.3 * l)
        proj = _matmul(h, w_in, l, F32, IN_TN, "in_proj")
        proj_p = proj[:T_PROMPT]
        ks_out.append(proj_p[:, COL_DK * LANE:COL_DV * LANE]
                      .reshape(BATCH, SEQ, DIFF_HEADS, 2, DIFF_DH))
        vs_out.append(proj_p[:, COL_DV * LANE:COL_CB * LANE]
                      .reshape(BATCH, SEQ, DIFF_HEADS, 2 * DIFF_DH))
        ret_p, s_fin = _retention(proj, ret_decay_logit, l, use_ctx=False)
        ret_s = _retention(proj, ret_decay_logit, l, use_ctx=True, state_ret=state_ret,
                           tables=ret_tables)
        ss_out.append(s_fin)
        diff_p = _diff_attention(proj, diff_lambda, l, lam_init, use_ctx=False)
        diff_s = _diff_attention(proj, diff_lambda, l, lam_init, use_ctx=True, cache_k=cache_k,
                                 cache_v=cache_v, tables=diff_tables)
        conv_p = _gated_conv(proj, conv_w_t, l, use_ctx=False)
        conv_s = _gated_conv(proj, conv_w_t, l, use_ctx=True)
        four_p = _fourier(proj, cw, sw, cs_p, use_ctx=False)
        four_s = _fourier(proj, cw, sw, cs_s, use_ctx=True)
        merged = _merge(h, [ret_p, diff_p, conv_p, four_p], [ret_s, diff_s, conv_s, four_s],
                        w_mgate, b_mgate, w_branch, l)
        mix = _matmul(merged, w_out, l, F32, MM_TN, "out_proj")
        x, h2 = _resid_ln(x, mix, mods[l], 2, ln1_g, ln1_b, l, mods[l], 4, 3, F32)
        idx, wgt, rank, cnt = _router(h2, w_router_pad, b_router_pad, l)
        dest, groups, nvalid, last_tile, padded = _dispatch_plan(idx, rank, cnt)
        xs = _dispatch(h2, dest, last_tile, padded, nvalid)
        y = _experts(xs, groups, nvalid, w_gu, b_gu, w_down, b_down, l)
        nxt = mods[min(l + 1, DEPTH - 1)]
        x, h = _combine_resid_ln(x, y, dest, wgt, mods[l], 5, ln2_g, ln2_b, l, nxt, 1, 0)
    y_p = x[:T_PROMPT].reshape(BATCH, SEQ, D_MODEL)
    y_s = x[T_PROMPT:].reshape(DEC_BATCH, DEC_SEQ, D_MODEL)
    return (y_p, y_s, jnp.stack(ks_out, axis=1), jnp.stack(vs_out, axis=1),
            jnp.stack(ss_out, axis=1))
```

```python
import functools
import math

import jax
import jax.numpy as jnp
import numpy as np
from jax import lax
from jax.experimental import pallas as pl
from jax.experimental.pallas import tpu as pltpu

D_MODEL = 2048
BATCH = 16
SEQ = 256
DEPTH = 4
DEC_BATCH = 4
DEC_SEQ = 2048
PAST_LEN = 512
GRID_W = 64
N_BRANCH = 4
BRANCH_W = 512
RET_HEADS = 4
RET_DK = 128
RET_CHUNK = 128
DIFF_HEADS = 4
DIFF_DH = 64
N_EXPERTS = 32
TOP_K = 4
D_EXPERT = 2048
SWIGLU_ALPHA = 1.702
SWIGLU_LIMIT = 7.0
ROPE_BASE = 10000.0
LN_EPS = 1e-5
RMS_EPS = 1e-6
DEEP_ALPHA = (2 * DEPTH) ** 0.25
D_IN = 5632

LANE = 128
T_PROMPT = BATCH * SEQ
T_SAMPLE = DEC_BATCH * DEC_SEQ
T_ALL = T_PROMPT + T_SAMPLE
N_COND = 1 + DEC_BATCH
COND_PAD = 8

ROW_TILE = 256
MM_TM = 512
MM_TN = 512
IN_TN = 1408
MERGE_TN = 256
MOE_TM = 256
MOE_S = 8
MOE_FC = 256
N_CHUNK = D_EXPERT // MOE_FC
MOE_GROUP = 4
DOWN_BLOCK_ROWS = 512
X_AHEAD = 2
X_SLOTS = X_AHEAD + 1
CAST_ROWS = 256
P_PAIRS = T_ALL * TOP_K
P_PAD = P_PAIRS + N_EXPERTS * MOE_TM
N_MOE_TILES = P_PAD // MOE_TM
G_MAX = N_EXPERTS + N_MOE_TILES // MOE_S
VMEM_LIMIT = 56 * 1024 * 1024

COL_RQ, COL_RK, COL_RV, COL_RG = 0, 4, 8, 12
COL_DQ, COL_DK, COL_DV = 16, 20, 24
COL_CB, COL_CC, COL_CX = 28, 32, 36
COL_FX = 40

BF16 = jnp.bfloat16
F32 = jnp.float32


def _params(*sem):
    return pltpu.CompilerParams(dimension_semantics=sem, vmem_limit_bytes=VMEM_LIMIT)


def _cond_row_of_tile(i, tile):
    start = i * tile
    return jnp.where(start < T_PROMPT, 0, 1 + (start - T_PROMPT) // DEC_SEQ)


def _cache_bf16(src_ref, dst_ref):
    n = src_ref.shape[0] // CAST_ROWS

    def body(i, carry):
        r = pl.multiple_of(i * CAST_ROWS, CAST_ROWS)
        dst_ref[pl.ds(r, CAST_ROWS), :] = src_ref[pl.ds(r, CAST_ROWS), :].astype(BF16)
        return carry

    lax.fori_loop(0, n, body, 0)


def _mod_kernel(c_ref, w_ref, b_ref, o_ref):
    c = c_ref[...]
    a = (c * jax.nn.sigmoid(c)).astype(BF16)
    o_ref[...] = jnp.dot(a, w_ref[...].astype(BF16), preferred_element_type=F32) + b_ref[...]


def _modulation(cond_pad, w_mod, b_mod, layer):
    n = 6 * D_MODEL
    return pl.pallas_call(
        _mod_kernel,
        out_shape=jax.ShapeDtypeStruct((COND_PAD, n), F32),
        grid=(n // MM_TN,),
        in_specs=[
            pl.BlockSpec((COND_PAD, D_MODEL), lambda j: (0, 0)),
            pl.BlockSpec((None, D_MODEL, MM_TN), lambda j: (layer, 0, j)),
            pl.BlockSpec((None, 1, MM_TN), lambda j: (layer, 0, j)),
        ],
        out_specs=pl.BlockSpec((COND_PAD, MM_TN), lambda j: (0, j)),
        compiler_params=_params("arbitrary"),
        name="modulation",
    )(cond_pad, w_mod, b_mod.reshape(DEPTH, 1, n))


def _ln(x):
    xc = x - jnp.mean(x, axis=-1, keepdims=True)
    return xc * lax.rsqrt(jnp.mean(xc * xc, axis=-1, keepdims=True) + LN_EPS)


def _ln_mod_kernel(x_ref, sc_ref, sh_ref, h_ref):
    h_ref[...] = (_ln(x_ref[...]) * (1.0 + sc_ref[...]) + sh_ref[...]).astype(h_ref.dtype)


def _mod_spec(which):
    return pl.BlockSpec((None, None, 1, D_MODEL),
                        lambda i, *_: (_cond_row_of_tile(i, ROW_TILE), which, 0, 0))


def _ln_mod(x, mod4, sc_idx, sh_idx):
    return pl.pallas_call(
        _ln_mod_kernel,
        out_shape=jax.ShapeDtypeStruct((T_ALL, D_MODEL), BF16),
        grid=(T_ALL // ROW_TILE,),
        in_specs=[pl.BlockSpec((ROW_TILE, D_MODEL), lambda i: (i, 0)),
                  _mod_spec(sc_idx), _mod_spec(sh_idx)],
        out_specs=pl.BlockSpec((ROW_TILE, D_MODEL), lambda i: (i, 0)),
        compiler_params=_params("parallel"),
        name="ln_mod",
    )(x, mod4, mod4)


def _resid_ln_kernel(x_ref, y_ref, g_ref, lg_ref, lb_ref, sc_ref, sh_ref, xo_ref, h_ref):
    z = DEEP_ALPHA * x_ref[...] + g_ref[...] * y_ref[...]
    xn = _ln(z) * lg_ref[...] + lb_ref[...]
    xo_ref[...] = xn
    h_ref[...] = (_ln(xn) * (1.0 + sc_ref[...]) + sh_ref[...]).astype(h_ref.dtype)


def _resid_ln(x, y, mod_gate, gate_idx, ln_g, ln_b, layer, mod_next, sc_idx, sh_idx, h_dtype):
    row = pl.BlockSpec((ROW_TILE, D_MODEL), lambda i: (i, 0))
    vec = pl.BlockSpec((None, 1, D_MODEL), lambda i: (layer, 0, 0))
    return pl.pallas_call(
        _resid_ln_kernel,
        out_shape=(jax.ShapeDtypeStruct((T_ALL, D_MODEL), F32),
                   jax.ShapeDtypeStruct((T_ALL, D_MODEL), h_dtype)),
        grid=(T_ALL // ROW_TILE,),
        in_specs=[row, row, _mod_spec(gate_idx), vec, vec, _mod_spec(sc_idx), _mod_spec(sh_idx)],
        out_specs=(row, row),
        compiler_params=_params("parallel"),
        name="resid_ln",
    )(x, y, mod_gate, ln_g.reshape(DEPTH, 1, D_MODEL), ln_b.reshape(DEPTH, 1, D_MODEL),
      mod_next, mod_next)


def _mm_kernel(x_ref, w_ref, o_ref, wb_ref):
    @pl.when(pl.program_id(1) == 0)
    def _():
        _cache_bf16(w_ref, wb_ref)

    o_ref[...] = jnp.dot(x_ref[...], wb_ref[...], preferred_element_type=F32).astype(o_ref.dtype)


def _matmul(x, w, layer, out_dtype, tn, name):
    m, k = x.shape
    n = w.shape[-1]
    return pl.pallas_call(
        _mm_kernel,
        out_shape=jax.ShapeDtypeStruct((m, n), out_dtype),
        grid=(n // tn, m // MM_TM),
        in_specs=[pl.BlockSpec((MM_TM, k), lambda j, i: (i, 0)),
                  pl.BlockSpec((None, k, tn), lambda j, i: (layer, 0, j))],
        out_specs=pl.BlockSpec((MM_TM, tn), lambda j, i: (i, j)),
        scratch_shapes=[pltpu.VMEM((k, tn), BF16)],
        compiler_params=_params("arbitrary", "arbitrary"),
        name=name,
    )(x, w)


def _rope_tables(d):
    rows = DEC_SEQ // GRID_W
    row = jnp.repeat(jnp.arange(rows, dtype=F32), GRID_W)
    col = jnp.tile(jnp.arange(GRID_W, dtype=F32), rows)
    axis_dim = d // 2
    inv_freq = ROPE_BASE ** (-jnp.arange(0, axis_dim, 2, dtype=F32) / axis_dim)
    ang_r = row[:, None] * inv_freq
    ang_c = col[:, None] * inv_freq
    cos = jnp.concatenate([jnp.cos(ang_r), jnp.cos(ang_r), jnp.cos(ang_c), jnp.cos(ang_c)], -1)
    sin = jnp.concatenate([-jnp.sin(ang_r), jnp.sin(ang_r), -jnp.sin(ang_c), jnp.sin(ang_c)], -1)
    reps = LANE // d
    return jnp.tile(cos, (1, reps)), jnp.tile(sin, (1, reps))


def _rope(x, cos, sin, d):
    q = d // 4
    lane = lax.broadcasted_iota(jnp.int32, x.shape, 1)
    first = (lane % (2 * q)) < q
    partner = jnp.where(first, pltpu.roll(x, LANE - q, 1), pltpu.roll(x, q, 1))
    return x * cos + partner * sin


def _retention_kernel(*refs, seq, use_ctx):
    if use_ctx:
        (q_ref, k_ref, v_ref, g_ref, lgt_ref, s0_ref, cos_ref, sin_ref, o_ref,
         of_ref, ob_ref) = refs
    else:
        q_ref, k_ref, v_ref, g_ref, lgt_ref, o_ref, sfin_ref, of_ref, ob_ref = refs
    c = RET_CHUNK
    n_chunk = seq // c
    ii = lax.broadcasted_iota(jnp.int32, (c, c), 0).astype(F32)
    jj = lax.broadcasted_iota(jnp.int32, (c, c), 1).astype(F32)
    col_pos = lax.broadcasted_iota(jnp.int32, (c, 1), 0).astype(F32)
    row_pos = lax.broadcasted_iota(jnp.int32, (1, c), 1).astype(F32)

    def direction(d):
        logit = lgt_ref[d]
        lg = jnp.minimum(logit, 0.0) - jnp.log1p(jnp.exp(-jnp.abs(logit)))
        lg_mat = jnp.broadcast_to(lg[0:1, :], (c, c))
        lg_col = lg[:, 0:1][0:1, :]
        if d == 0:
            dist = ii - jj
            kw = jnp.exp((c - 1 - row_pos) * lg_col)
            qw = jnp.exp((col_pos + 1.0) * lg_col)
        else:
            dist = jj - ii
            kw = jnp.exp(row_pos * lg_col)
            qw = jnp.exp((c - col_pos) * lg_col)
        decay = jnp.where(dist >= 0, jnp.exp(jnp.where(dist >= 0, dist, 0.0) * lg_mat), 0.0)
        chunk_decay = jnp.exp(c * lg_mat)
        return decay, kw, qw, chunk_decay

    def chunk(n):
        rows = slice(n * c, (n + 1) * c)
        q = q_ref[rows, :]
        k = k_ref[rows, :] * (RET_DK ** -0.5)
        if use_ctx:
            q = _rope(q, cos_ref[rows, :], sin_ref[rows, :], RET_DK)
            k = _rope(k, cos_ref[rows, :], sin_ref[rows, :], RET_DK)
        return q, k.T, v_ref[rows, :].astype(BF16)

    for d, acc_ref in ((0, of_ref), (1, ob_ref)):
        decay, kw, qw, chunk_decay = direction(d)
        if use_ctx:
            state = s0_ref[d]
        else:
            state = jnp.zeros((RET_DK, RET_DK), F32)
        order = range(n_chunk) if d == 0 else range(n_chunk - 1, -1, -1)
        for n in order:
            q, kt, v = chunk(n)
            scores = jnp.dot(q.astype(BF16), kt.astype(BF16), preferred_element_type=F32) * decay
            o = jnp.dot(scores.astype(BF16), v, preferred_element_type=F32)
            o = o + jnp.dot((q * qw).astype(BF16), state.astype(BF16), preferred_element_type=F32)
            acc_ref[n * c:(n + 1) * c, :] = o
            state = chunk_decay * state + jnp.dot((kt * kw).astype(BF16), v,
                                                  preferred_element_type=F32)
        if not use_ctx:
            sfin_ref[d] = state

    o = of_ref[...] + ob_ref[...]
    o = o * lax.rsqrt(jnp.mean(o * o, axis=-1, keepdims=True) + RMS_EPS)
    g = g_ref[...]
    o_ref[...] = (g * jax.nn.sigmoid(g) * o).astype(o_ref.dtype)


def _retention(proj, decay_logit, layer, *, use_ctx, state_ret=None, tables=None):
    seq, nb, row0 = (DEC_SEQ, DEC_BATCH, T_PROMPT // DEC_SEQ) if use_ctx else (SEQ, BATCH, 0)

    def col(c0):
        return pl.BlockSpec((seq, LANE), lambda b, h: (row0 + b, c0 + h))

    lgt = jnp.broadcast_to(decay_logit[layer][:, :, None, None], (2, RET_HEADS, 8, LANE))
    in_specs = [col(COL_RQ), col(COL_RK), col(COL_RV), col(COL_RG),
                pl.BlockSpec((2, None, 8, LANE), lambda b, h: (0, h, 0, 0))]
    args = [proj, proj, proj, proj, lgt]
    out_branch = jax.ShapeDtypeStruct((nb * seq, BRANCH_W), BF16)
    branch_spec = pl.BlockSpec((seq, LANE), lambda b, h: (b, h))
    if use_ctx:
        in_specs += [pl.BlockSpec((None, None, 2, None, RET_DK, RET_DK),
                                  lambda b, h: (b, layer, 0, h, 0, 0)),
                     pl.BlockSpec((seq, LANE), lambda b, h: (0, 0)),
                     pl.BlockSpec((seq, LANE), lambda b, h: (0, 0))]
        args += [state_ret, tables[0], tables[1]]
        out_shape, out_specs = out_branch, branch_spec
    else:
        out_shape = (out_branch,
                     jax.ShapeDtypeStruct((nb, 2, RET_HEADS, RET_DK, RET_DK), F32))
        out_specs = (branch_spec,
                     pl.BlockSpec((None, 2, None, RET_DK, RET_DK), lambda b, h: (b, 0, h, 0, 0)))
    return pl.pallas_call(
        functools.partial(_retention_kernel, seq=seq, use_ctx=use_ctx),
        out_shape=out_shape,
        grid=(nb, RET_HEADS),
        in_specs=in_specs,
        out_specs=out_specs,
        scratch_shapes=[pltpu.VMEM((seq, LANE), F32), pltpu.VMEM((seq, LANE), F32)],
        compiler_params=_params("parallel", "parallel"),
        name="retention_ctx" if use_ctx else "retention",
    )(*args)


def _diff_attn_kernel(*refs, seq, tq, use_ctx, lam_init):
    if use_ctx:
        (q_ref, k_ref, v_ref, lam_ref, kc_ref, vc_ref, cosq_ref, sinq_ref, cosk_ref, sink_ref,
         o_ref, kt_ref, va_ref) = refs
    else:
        q_ref, k_ref, v_ref, lam_ref, o_ref, kt_ref, va_ref = refs
    c = LANE

    @pl.when(pl.program_id(2) == 0)
    def _():
        for n in range(seq // c):
            rows = slice(n * c, (n + 1) * c)
            k = k_ref[rows, :]
            if use_ctx:
                k = _rope(k, cosk_ref[rows, :], sink_ref[rows, :], DIFF_DH)
            kt_ref[:, rows] = k.T.astype(BF16)
        va_ref[0:seq, :] = v_ref[...].astype(BF16)
        if use_ctx:
            for n in range(PAST_LEN // c):
                rows = slice(n * c, (n + 1) * c)
                kt_ref[:, seq + n * c:seq + (n + 1) * c] = kc_ref[rows, :].T.astype(BF16)
            va_ref[seq:seq + PAST_LEN, :] = vc_ref[...].astype(BF16)

    lam_v = lam_ref[...]
    lam = (jnp.exp(jnp.sum(lam_v[0:1] * lam_v[1:2], axis=-1, keepdims=True))
           - jnp.exp(jnp.sum(lam_v[2:3] * lam_v[3:4], axis=-1, keepdims=True)) + lam_init)
    q = q_ref[...]
    if use_ctx:
        q = _rope(q, cosq_ref[...], sinq_ref[...], DIFF_DH)
    lane = lax.broadcasted_iota(jnp.int32, q.shape, 1)
    q = q * (DIFF_DH ** -0.5)
    kt = kt_ref[...]
    probs = []
    for m in range(2):
        qm = jnp.where((lane < DIFF_DH) == (m == 0), q, 0.0).astype(BF16)
        s = jnp.dot(qm, kt, preferred_element_type=F32)
        p = jnp.exp(s - jnp.max(s, axis=-1, keepdims=True))
        inv = 1.0 / jnp.sum(p, axis=-1, keepdims=True)
        probs.append((p, inv))
    w = probs[0][0] * probs[0][1] - probs[1][0] * (lam * probs[1][1])
    o = jnp.dot(w.astype(BF16), va_ref[...], preferred_element_type=F32)
    o = o * lax.rsqrt(jnp.mean(o * o, axis=-1, keepdims=True) + RMS_EPS) * (1.0 - lam_init)
    o_ref[...] = o.astype(o_ref.dtype)


def _diff_attention(proj, diff_lambda, layer, lam_init, *, use_ctx, cache_k=None, cache_v=None,
                    tables=None):
    seq, nb, row0 = (DEC_SEQ, DEC_BATCH, T_PROMPT // DEC_SEQ) if use_ctx else (SEQ, BATCH, 0)
    tq = 256
    n_q = seq // tq
    n_keys = seq + (PAST_LEN if use_ctx else 0)
    in_specs = [pl.BlockSpec((tq, LANE), lambda b, h, i: ((row0 + b) * n_q + i, COL_DQ + h)),
                pl.BlockSpec((seq, LANE), lambda b, h, i: (row0 + b, COL_DK + h)),
                pl.BlockSpec((seq, LANE), lambda b, h, i: (row0 + b, COL_DV + h)),
                pl.BlockSpec((None, 4, DIFF_DH), lambda b, h, i: (layer, 0, 0))]
    args = [proj, proj, proj, diff_lambda]
    if use_ctx:
        ctx_spec = pl.BlockSpec((None, None, PAST_LEN, LANE), lambda b, h, i: (b, layer, 0, h))
        full = pl.BlockSpec((seq, LANE), lambda b, h, i: (0, 0))
        qtab = pl.BlockSpec((tq, LANE), lambda b, h, i: (i, 0))
        in_specs += [ctx_spec, ctx_spec, qtab, qtab, full, full]
        args += [cache_k.reshape(DEC_BATCH, DEPTH, PAST_LEN, DIFF_HEADS * 2 * DIFF_DH),
                 cache_v.reshape(DEC_BATCH, DEPTH, PAST_LEN, DIFF_HEADS * 2 * DIFF_DH),
                 tables[0], tables[1], tables[0], tables[1]]
    return pl.pallas_call(
        functools.partial(_diff_attn_kernel, seq=seq, tq=tq, use_ctx=use_ctx, lam_init=lam_init),
        out_shape=jax.ShapeDtypeStruct((nb * seq, BRANCH_W), BF16),
        grid=(nb, DIFF_HEADS, n_q),
        in_specs=in_specs,
        out_specs=pl.BlockSpec((tq, LANE), lambda b, h, i: (b * n_q + i, h)),
        scratch_shapes=[pltpu.VMEM((LANE, n_keys), BF16), pltpu.VMEM((n_keys, LANE), BF16)],
        compiler_params=_params("parallel", "parallel", "arbitrary"),
        name="diff_attention_ctx" if use_ctx else "diff_attention",
    )(*args)


def _conv_kernel(cb_ref, cc_ref, cx_ref, w_ref, o_ref, *, seq):
    u = cc_ref[...] * cx_ref[...]
    row = lax.broadcasted_iota(jnp.int32, u.shape, 0)
    prev = jnp.where(row == 0, 0.0, pltpu.roll(u, 1, 0))
    nxt = jnp.where(row == seq - 1, 0.0, pltpu.roll(u, seq - 1, 0))
    w = w_ref[...]
    conv = prev * w[0:1, :] + u * w[1:2, :] + nxt * w[2:3, :]
    o_ref[...] = (cb_ref[...] * conv).astype(o_ref.dtype)


def _gated_conv(proj, conv_w_t, layer, *, use_ctx):
    seq, nb, row0 = (DEC_SEQ, DEC_BATCH, T_PROMPT // DEC_SEQ) if use_ctx else (SEQ, BATCH, 0)
    n_c = BRANCH_W // LANE

    def col(c0):
        return pl.BlockSpec((seq, LANE), lambda b, c: (row0 + b, c0 + c))

    return pl.pallas_call(
        functools.partial(_conv_kernel, seq=seq),
        out_shape=jax.ShapeDtypeStruct((nb * seq, BRANCH_W), BF16),
        grid=(nb, n_c),
        in_specs=[col(COL_CB), col(COL_CC), col(COL_CX),
                  pl.BlockSpec((None, 3, LANE), lambda b, c: (layer, 0, c))],
        out_specs=pl.BlockSpec((seq, LANE), lambda b, c: (b, c)),
        compiler_params=_params("parallel", "parallel"),
        name="gated_conv_ctx" if use_ctx else "gated_conv",
    )(proj, proj, proj, conv_w_t)


def _dft_tables(n):
    j = jnp.arange(n, dtype=jnp.int32)
    m = (j[:, None] * j[None, :]) % n
    ang = m.astype(F32) * (2.0 * math.pi / n)
    return jnp.cos(ang), jnp.sin(ang)


def _fourier_kernel(x_ref, cw_ref, sw_ref, cs_ref, o_ref, y_ref, *, seq):
    @pl.when(pl.program_id(1) == 0)
    def _():
        for g in range(BRANCH_W // LANE):
            cols = slice(g * LANE, (g + 1) * LANE)
            xg = x_ref[:, cols].astype(BF16)
            y_ref[0:seq, cols] = jnp.dot(xg, cw_ref[...], preferred_element_type=F32).astype(BF16)
            y_ref[seq:2 * seq, cols] = (-jnp.dot(xg, sw_ref[...],
                                                 preferred_element_type=F32)).astype(BF16)

    o = jnp.dot(cs_ref[...], y_ref[...], preferred_element_type=F32)
    o_ref[...] = (o * ((seq * LANE) ** -0.5)).astype(o_ref.dtype)


def _fourier(proj, cw, sw, cs_l, *, use_ctx):
    seq, nb, row0 = (DEC_SEQ, DEC_BATCH, T_PROMPT // DEC_SEQ) if use_ctx else (SEQ, BATCH, 0)
    tl = 256
    n_t = seq // tl
    wide = BRANCH_W // LANE
    return pl.pallas_call(
        functools.partial(_fourier_kernel, seq=seq),
        out_shape=jax.ShapeDtypeStruct((nb * seq, BRANCH_W), BF16),
        grid=(nb, n_t),
        in_specs=[pl.BlockSpec((seq, BRANCH_W), lambda b, i: (row0 + b, COL_FX // wide)),
                  pl.BlockSpec((LANE, LANE), lambda b, i: (0, 0)),
                  pl.BlockSpec((LANE, LANE), lambda b, i: (0, 0)),
                  pl.BlockSpec((tl, 2 * seq), lambda b, i: (i, 0))],
        out_specs=pl.BlockSpec((tl, BRANCH_W), lambda b, i: (b * n_t + i, 0)),
        scratch_shapes=[pltpu.VMEM((2 * seq, BRANCH_W), BF16)],
        compiler_params=_params("parallel", "arbitrary"),
        name="fourier_ctx" if use_ctx else "fourier",
    )(proj, cw, sw, cs_l)


def _merge_kernel(h_ref, *refs):
    ctx_refs = refs[0:N_BRANCH]
    lat_refs = refs[N_BRANCH:2 * N_BRANCH]
    wg_refs = refs[2 * N_BRANCH:3 * N_BRANCH]
    bg_refs = refs[3 * N_BRANCH:4 * N_BRANCH]
    wb_ref, o_ref, wgb_ref, wbb_ref = refs[4 * N_BRANCH:]
    is_ctx = pl.program_id(1) < T_PROMPT // MM_TM

    @pl.when(pl.program_id(1) == 0)
    def _():
        for n in range(N_BRANCH):
            _cache_bf16(wg_refs[n], wgb_ref.at[n])
            wbb_ref[n] = wb_ref[n].astype(BF16)

    h = h_ref[...]
    acc = jnp.zeros(o_ref.shape, F32)
    for n in range(N_BRANCH):
        gate = jax.nn.sigmoid(jnp.dot(h, wgb_ref[n], preferred_element_type=F32) + bg_refs[n][...])
        branch = jnp.where(is_ctx, ctx_refs[n][...], lat_refs[n][...])
        acc = acc + gate * jnp.dot(branch, wbb_ref[n], preferred_element_type=F32)
    o_ref[...] = acc.astype(o_ref.dtype)


def _merge(h, ctx_branches, lat_branches, w_mgate, b_mgate, w_branch, layer):
    tm, tn = MM_TM, MERGE_TN
    n_t = D_MODEL // tn
    n_ctx = T_PROMPT // tm
    ctx_spec = pl.BlockSpec((tm, BRANCH_W), lambda j, i: (jnp.minimum(i, n_ctx - 1), 0))
    lat_spec = pl.BlockSpec((tm, BRANCH_W), lambda j, i: (jnp.maximum(i - n_ctx, 0), 0))

    def gate_w(n):
        return pl.BlockSpec((None, D_MODEL, tn), lambda j, i: (layer, 0, n * n_t + j))

    def gate_b(n):
        return pl.BlockSpec((None, 1, tn), lambda j, i: (layer, 0, n * n_t + j))

    return pl.pallas_call(
        _merge_kernel,
        out_shape=jax.ShapeDtypeStruct((T_ALL, D_MODEL), BF16),
        grid=(n_t, T_ALL // tm),
        in_specs=[pl.BlockSpec((tm, D_MODEL), lambda j, i: (i, 0))]
        + [ctx_spec] * N_BRANCH + [lat_spec] * N_BRANCH
        + [gate_w(n) for n in range(N_BRANCH)]
        + [gate_b(n) for n in range(N_BRANCH)]
        + [pl.BlockSpec((None, N_BRANCH, BRANCH_W, tn), lambda j, i: (layer, 0, 0, j))],
        out_specs=pl.BlockSpec((tm, tn), lambda j, i: (i, j)),
        scratch_shapes=[pltpu.VMEM((N_BRANCH, D_MODEL, tn), BF16),
                        pltpu.VMEM((N_BRANCH, BRANCH_W, tn), BF16)],
        compiler_params=_params("arbitrary", "arbitrary"),
        name="merge",
    )(h, *ctx_branches, *lat_branches, *([w_mgate] * N_BRANCH),
      *([b_mgate.reshape(DEPTH, 1, N_BRANCH * D_MODEL)] * N_BRANCH), w_branch)


def _router_kernel(h_ref, w_ref, b_ref, idx_ref, wgt_ref, rank_ref, cnt_ref, run_ref):
    @pl.when(pl.program_id(0) == 0)
    def _():
        run_ref[...] = jnp.zeros(run_ref.shape, F32)

    logits = jnp.dot(h_ref[...].astype(BF16), w_ref[...].astype(BF16),
                     preferred_element_type=F32) + b_ref[...]
    lane = lax.broadcasted_iota(jnp.int32, logits.shape, 1).astype(F32)
    neg = jnp.float32(-jnp.inf)
    logits = jnp.where(lane < N_EXPERTS, logits, neg)
    idx_out = jnp.zeros(logits.shape, F32)
    val_out = jnp.zeros(logits.shape, F32)
    chosen = jnp.zeros(logits.shape, F32)
    picks = []
    top = None
    denom = None
    for k in range(TOP_K):
        m = jnp.max(logits, axis=-1, keepdims=True)
        idx = jnp.min(jnp.where(logits == m, lane, float(LANE)), axis=-1, keepdims=True)
        if k == 0:
            top = m
        e = jnp.exp(m - top)
        denom = e if k == 0 else denom + e
        idx_out = jnp.where(lane == k, idx, idx_out)
        val_out = jnp.where(lane == k, e, val_out)
        hit = lane == idx
        chosen = jnp.where(hit, 1.0, chosen)
        logits = jnp.where(hit, neg, logits)
        picks.append(hit)
    idx_ref[...] = idx_out.astype(jnp.int32)
    wgt_ref[...] = val_out / denom

    tm = logits.shape[0]
    earlier = (lax.broadcasted_iota(jnp.int32, (tm, tm), 1)
               < lax.broadcasted_iota(jnp.int32, (tm, tm), 0))
    before = jnp.dot(jnp.where(earlier, 1.0, 0.0).astype(BF16), chosen.astype(BF16),
                     preferred_element_type=F32) + run_ref[0:1, :]
    rank_out = jnp.zeros(logits.shape, F32)
    for k in range(TOP_K):
        r = jnp.sum(jnp.where(picks[k], before, 0.0), axis=-1, keepdims=True)
        rank_out = jnp.where(lane == k, r, rank_out)
    rank_ref[...] = rank_out.astype(jnp.int32)
    total = run_ref[0:1, :] + jnp.sum(chosen, axis=0, keepdims=True)
    run_ref[...] = jnp.broadcast_to(total, run_ref.shape)
    cnt_ref[...] = jnp.broadcast_to(total, cnt_ref.shape)


def _router(h, w_router_pad, b_router_pad, layer):
    row = pl.BlockSpec((ROW_TILE, LANE), lambda i: (i, 0))
    return pl.pallas_call(
        _router_kernel,
        out_shape=(jax.ShapeDtypeStruct((T_ALL, LANE), jnp.int32),
                   jax.ShapeDtypeStruct((T_ALL, LANE), F32),
                   jax.ShapeDtypeStruct((T_ALL, LANE), jnp.int32),
                   jax.ShapeDtypeStruct((8, LANE), F32)),
        grid=(T_ALL // ROW_TILE,),
        in_specs=[pl.BlockSpec((ROW_TILE, D_MODEL), lambda i: (i, 0)),
                  pl.BlockSpec((None, D_MODEL, LANE), lambda i: (layer, 0, 0)),
                  pl.BlockSpec((None, 1, LANE), lambda i: (layer, 0, 0))],
        out_specs=(row, row, row, pl.BlockSpec((8, LANE), lambda i: (0, 0))),
        scratch_shapes=[pltpu.VMEM((8, LANE), F32)],
        compiler_params=_params("arbitrary"),
        name="router",
    )(h, w_router_pad, b_router_pad)


def _expert_kernel(ge_ref, gt0_ref, gnt_ref, ng_ref, nvalid_ref, bgu_ref, bdn_ref, xs_ref, wgu_ref,
                   wdn_ref, y_ref, xland_ref, xbf_ref, acc_ref, wg_ref, wl_ref, wd_ref, wgb_ref,
                   wlb_ref, wdb_ref, out_ref, xsem, wsem, osem, zsem, *, layer):
    g = pl.program_id(0)
    e, t0, nt, ng = ge_ref[g], gt0_ref[g], gnt_ref[g], ng_ref[0]
    fc = MOE_FC

    def weight_copies(expert, c, slot):
        return (pltpu.make_async_copy(wgu_ref.at[layer, expert, :, pl.ds(c * fc, fc)],
                                      wg_ref.at[slot], wsem.at[slot]),
                pltpu.make_async_copy(wgu_ref.at[layer, expert, :, pl.ds(D_EXPERT + c * fc, fc)],
                                      wl_ref.at[slot], wsem.at[slot]),
                pltpu.make_async_copy(wdn_ref.at[layer, expert, pl.ds(c * fc, fc), :],
                                      wd_ref.at[slot], wsem.at[slot]))

    def tile_rows(tile):
        return pl.ds(pl.multiple_of(tile * MOE_TM, MOE_TM), MOE_TM)

    def x_copy(tile, slot):
        return pltpu.make_async_copy(xs_ref.at[tile_rows(tile)], xland_ref.at[slot], xsem.at[slot])

    def y_copy(tile, slot):
        return pltpu.make_async_copy(out_ref.at[slot], y_ref.at[tile_rows(tile)], osem.at[slot])

    def start_first_x(first_tile, n_tiles):
        for j in range(X_AHEAD):
            @pl.when(j < n_tiles)
            def _():
                x_copy(first_tile + j, j).start()

    @pl.when(g == 0)
    def _():
        out_ref[0] = jnp.zeros(out_ref.shape[1:], out_ref.dtype)
        for do in ("start", "wait"):
            for t in range(P_PAIRS // MOE_TM, N_MOE_TILES):
                @pl.when(t >= nvalid_ref[0])
                def _():
                    getattr(pltpu.make_async_copy(out_ref.at[0], y_ref.at[pl.ds(t * MOE_TM, MOE_TM)],
                                                  zsem), do)()
        for cp in weight_copies(e, 0, 0):
            cp.start()
        start_first_x(t0, nt)

    @pl.when(g < ng)
    def _():
        for c in range(N_CHUNK):
            slot = c % 2
            last = c + 1 == N_CHUNK
            for cp in weight_copies(e, c, slot):
                cp.wait()
            if not last:
                for cp in weight_copies(e, c + 1, 1 - slot):
                    cp.start()
            else:
                @pl.when(g + 1 < ng)
                def _():
                    for cp in weight_copies(ge_ref[g + 1], 0, 0):
                        cp.start()
                    start_first_x(gt0_ref[g + 1], gnt_ref[g + 1])
            _cache_bf16(wg_ref.at[slot], wgb_ref)
            _cache_bf16(wl_ref.at[slot], wlb_ref)
            wdb_ref[...] = wd_ref[slot].astype(BF16)
            bg = bgu_ref[:, c * fc:(c + 1) * fc]
            bl = bgu_ref[:, D_EXPERT + c * fc:D_EXPERT + (c + 1) * fc]

            def tiles(m0, n_tile, c=c, last=last, bg=bg, bl=bl):
                if c == 0:
                    for j in range(n_tile):
                        m = m0 + j
                        xslot = m % X_SLOTS
                        x_copy(t0 + m, xslot).wait()

                        @pl.when(m + X_AHEAD < nt)
                        def _():
                            x_copy(t0 + m + X_AHEAD, (m + X_AHEAD) % X_SLOTS).start()

                        xbf_ref[pl.ds(pl.multiple_of(m * MOE_TM, MOE_TM), MOE_TM), :] = (
                            xland_ref[xslot].astype(BF16))
                rows = pl.ds(pl.multiple_of(m0 * MOE_TM, MOE_TM), n_tile * MOE_TM)
                x = xbf_ref[rows, :]
                gate = jnp.dot(x, wgb_ref[...], preferred_element_type=F32) + bg
                lin = jnp.dot(x, wlb_ref[...], preferred_element_type=F32) + bl
                gate = jnp.minimum(gate, SWIGLU_LIMIT)
                lin = jnp.clip(lin, -SWIGLU_LIMIT, SWIGLU_LIMIT)
                act = ((lin + 1.0) * gate * jax.nn.sigmoid(SWIGLU_ALPHA * gate)).astype(BF16)
                if last:
                    done = acc_ref[rows, :] + jnp.dot(act, wdb_ref[...], preferred_element_type=F32)
                    for j in range(n_tile):
                        m = m0 + j
                        oslot = m % 2

                        @pl.when(m >= 2)
                        def _():
                            y_copy(t0, oslot).wait()

                        out_ref[oslot] = done[j * MOE_TM:(j + 1) * MOE_TM]
                        y_copy(t0 + m, oslot).start()
                else:
                    n_split = max(1, n_tile * MOE_TM // DOWN_BLOCK_ROWS)
                    width = D_MODEL // n_split
                    for s in range(n_split):
                        cols = slice(s * width, (s + 1) * width)
                        part = jnp.dot(act, wdb_ref[:, cols], preferred_element_type=F32)
                        if c == 0:
                            acc_ref[rows, cols] = part + bdn_ref[:, cols]
                        else:
                            acc_ref[rows, cols] += part

            def run(group, c=c, last=last):
                n_run = nt // group

                def body(i, carry):
                    tiles(group * i, group)
                    return carry

                lax.fori_loop(0, n_run, body, 0)
                size, done_tiles = group // 2, n_run * group
                while size >= 1:
                    @pl.when((nt - done_tiles) // size % 2 == 1)
                    def _(size=size):
                        tiles(nt - (nt - done_tiles) % (2 * size), size)
                    size //= 2

            run(2 if last else MOE_GROUP)

        @pl.when(nt >= 2)
        def _():
            y_copy(t0, nt % 2).wait()

        y_copy(t0, (nt - 1) % 2).wait()


def _experts(xs, groups, nvalid, w_gu, b_gu, w_down, b_down, layer):
    g_expert, g_tile0, g_tiles, n_groups = groups
    any_spec = pl.BlockSpec(memory_space=pl.ANY)
    return pl.pallas_call(
        functools.partial(_expert_kernel, layer=layer),
        out_shape=jax.ShapeDtypeStruct((P_PAD, D_MODEL), F32),
        grid_spec=pltpu.PrefetchScalarGridSpec(
            num_scalar_prefetch=5,
            grid=(G_MAX,),
            in_specs=[
                pl.BlockSpec((None, None, 1, 2 * D_EXPERT), lambda g, ge, *_: (layer, ge[g], 0, 0)),
                pl.BlockSpec((None, None, 1, D_MODEL), lambda g, ge, *_: (layer, ge[g], 0, 0)),
                any_spec, any_spec, any_spec,
            ],
            out_specs=any_spec,
            scratch_shapes=[
                pltpu.VMEM((X_SLOTS, MOE_TM, D_MODEL), F32),
                pltpu.VMEM((MOE_S * MOE_TM, D_MODEL), BF16),
                pltpu.VMEM((MOE_S * MOE_TM, D_MODEL), F32),
                pltpu.VMEM((2, D_MODEL, MOE_FC), F32),
                pltpu.VMEM((2, D_MODEL, MOE_FC), F32),
                pltpu.VMEM((2, MOE_FC, D_MODEL), F32),
                pltpu.VMEM((D_MODEL, MOE_FC), BF16),
                pltpu.VMEM((D_MODEL, MOE_FC), BF16),
                pltpu.VMEM((MOE_FC, D_MODEL), BF16),
                pltpu.VMEM((2, MOE_TM, D_MODEL), F32),
                pltpu.SemaphoreType.DMA((X_SLOTS,)), pltpu.SemaphoreType.DMA((2,)),
                pltpu.SemaphoreType.DMA((2,)), pltpu.SemaphoreType.DMA(()),
            ],
        ),
        compiler_params=_params("arbitrary"),
        name="experts",
    )(g_expert, g_tile0, g_tiles, n_groups, nvalid,
      b_gu.reshape(DEPTH, N_EXPERTS, 1, 2 * D_EXPERT), b_down.reshape(DEPTH, N_EXPERTS, 1, D_MODEL),
      xs, w_gu, w_down)


def _dispatch_plan(idx, rank, cnt):
    counts = cnt[0, :N_EXPERTS].astype(jnp.int32)
    padded = ((counts + MOE_TM - 1) // MOE_TM) * MOE_TM
    gend = jnp.cumsum(padded)
    gstart = gend - padded
    top_i = idx[:, :TOP_K]
    experts = jnp.arange(N_EXPERTS, dtype=jnp.int32)
    start_of = jnp.sum(jnp.where(top_i[:, :, None] == experts, gstart, 0), axis=-1)
    dest = (start_of + rank[:, :TOP_K]).reshape(-1)
    nvalid = gend[-1:] // MOE_TM
    last_tile = gend // MOE_TM - 1
    n_tile = padded // MOE_TM
    n_pass = (n_tile + MOE_S - 1) // MOE_S
    pass_end = jnp.cumsum(n_pass)
    g = jnp.arange(G_MAX, dtype=jnp.int32)
    g_expert = jnp.minimum(jnp.sum((pass_end[None, :] <= g[:, None]).astype(jnp.int32), axis=1),
                           N_EXPERTS - 1)
    onehot = g_expert[:, None] == experts[None, :]

    def of_expert(v):
        return jnp.sum(jnp.where(onehot, v[None, :], 0), axis=1)

    k = g - of_expert(pass_end - n_pass)
    g_tile0 = of_expert(gstart // MOE_TM) + k * MOE_S
    g_tiles = jnp.where(g < pass_end[-1], jnp.clip(of_expert(n_tile) - k * MOE_S, 0, MOE_S), 0)
    groups = (g_expert, g_tile0, g_tiles, pass_end[-1:])
    return dest, groups, nvalid, last_tile, padded


def _row_copies(tile, dest_ref, src_row, dst_row, sem):
    base = tile * (ROW_TILE * TOP_K)

    def body(r, carry):
        for k in range(TOP_K):
            d = dest_ref[base + r * TOP_K + k]
            pltpu.make_async_copy(src_row(r, k, d), dst_row(r, k, d), sem).start(priority=k % 2)
        return carry

    lax.fori_loop(0, ROW_TILE, body, 0, unroll=8)


def _dispatch_kernel(dest_ref, last_tile_ref, padded_ref, nvalid_ref, h_ref, xs_ref, zero_ref,
                     sem, zsem):
    def zero_tile(t):
        rows = pl.ds(pl.multiple_of(t * MOE_TM, MOE_TM), MOE_TM)
        return pltpu.make_async_copy(zero_ref, xs_ref.at[rows], zsem)

    @pl.when(pl.program_id(0) == 0)
    def _():
        zero_ref[...] = jnp.zeros(zero_ref.shape, zero_ref.dtype)
        for do in ("start", "wait"):
            for e in range(N_EXPERTS):
                @pl.when(padded_ref[e] > 0)
                def _():
                    getattr(zero_tile(last_tile_ref[e]), do)()
            for t in range(P_PAIRS // MOE_TM, N_MOE_TILES):
                @pl.when(t >= nvalid_ref[0])
                def _():
                    getattr(zero_tile(t), do)()

    _row_copies(pl.program_id(0), dest_ref,
                lambda r, k, d: h_ref.at[pl.ds(r, 1)],
                lambda r, k, d: xs_ref.at[pl.ds(d, 1)], sem)
    for _ in range(TOP_K):
        pltpu.make_async_copy(h_ref, xs_ref.at[pl.ds(0, ROW_TILE)], sem).wait()


def _dispatch(h, dest, last_tile, padded, nvalid):
    return pl.pallas_call(
        _dispatch_kernel,
        out_shape=jax.ShapeDtypeStruct((P_PAD, D_MODEL), F32),
        grid_spec=pltpu.PrefetchScalarGridSpec(
            num_scalar_prefetch=4,
            grid=(T_ALL // ROW_TILE,),
            in_specs=[pl.BlockSpec((ROW_TILE, D_MODEL), lambda i, *_: (i, 0))],
            out_specs=pl.BlockSpec(memory_space=pl.ANY),
            scratch_shapes=[pltpu.VMEM((MOE_TM, D_MODEL), F32),
                            pltpu.SemaphoreType.DMA(()), pltpu.SemaphoreType.DMA(())],
        ),
        compiler_params=_params("arbitrary"),
        name="dispatch",
    )(dest, last_tile, padded, nvalid, h)


def _combine_kernel(dest_ref, x_ref, wgt_ref, g_ref, lg_ref, lb_ref, sc_ref, sh_ref, y_ref,
                    xo_ref, h_ref, ybuf_ref, sem):
    i = pl.program_id(0)

    def gather(tile, slot):
        _row_copies(tile, dest_ref,
                    lambda r, k, d: y_ref.at[pl.ds(d, 1)],
                    lambda r, k, d: ybuf_ref.at[slot, pl.ds(k * ROW_TILE + r, 1)],
                    sem.at[slot])

    @pl.when(i == 0)
    def _():
        gather(0, 0)

    @pl.when(i + 1 < pl.num_programs(0))
    def _():
        gather(i + 1, (i + 1) % 2)

    slot = i % 2
    pltpu.make_async_copy(y_ref.at[pl.ds(0, TOP_K * ROW_TILE)], ybuf_ref.at[slot],
                          sem.at[slot]).wait()
    wgt = wgt_ref[...]
    moe = jnp.zeros(x_ref.shape, F32)
    for k in range(TOP_K):
        moe = moe + wgt[:, k:k + 1] * ybuf_ref[slot, k * ROW_TILE:(k + 1) * ROW_TILE, :]
    z = DEEP_ALPHA * x_ref[...] + g_ref[...] * moe
    xn = _ln(z) * lg_ref[...] + lb_ref[...]
    xo_ref[...] = xn
    h_ref[...] = (_ln(xn) * (1.0 + sc_ref[...]) + sh_ref[...]).astype(h_ref.dtype)


def _combine_resid_ln(x, y, dest, wgt, mod_gate, gate_idx, ln_g, ln_b, layer, mod_next, sc_idx,
                      sh_idx):
    row = pl.BlockSpec((ROW_TILE, D_MODEL), lambda i, *_: (i, 0))
    vec = pl.BlockSpec((None, 1, D_MODEL), lambda i, *_: (layer, 0, 0))
    return pl.pallas_call(
        _combine_kernel,
        out_shape=(jax.ShapeDtypeStruct((T_ALL, D_MODEL), F32),
                   jax.ShapeDtypeStruct((T_ALL, D_MODEL), BF16)),
        grid_spec=pltpu.PrefetchScalarGridSpec(
            num_scalar_prefetch=1,
            grid=(T_ALL // ROW_TILE,),
            in_specs=[row, pl.BlockSpec((ROW_TILE, LANE), lambda i, *_: (i, 0)),
                      _mod_spec(gate_idx), vec, vec, _mod_spec(sc_idx), _mod_spec(sh_idx),
                      pl.BlockSpec(memory_space=pl.ANY)],
            out_specs=(row, row),
            scratch_shapes=[pltpu.VMEM((2, TOP_K * ROW_TILE, D_MODEL), F32),
                            pltpu.SemaphoreType.DMA((2,))],
        ),
        compiler_params=_params("arbitrary"),
        name="combine_resid_ln",
    )(dest, x, wgt, mod_gate, ln_g.reshape(DEPTH, 1, D_MODEL), ln_b.reshape(DEPTH, 1, D_MODEL),
      mod_next, mod_next, y)


def kernel(x_prompt, x_sample, c, cache_k, cache_v, state_ret, c_ctx, w_mod, b_mod, w_in,
           ret_decay_logit, diff_lambda, conv_w, w_branch, w_mgate, b_mgate, w_out,
           ln1_g, ln1_b, ln2_g, ln2_b, w_router, b_router, w_gu, b_gu, w_down, b_down):
    x = jnp.concatenate([x_prompt.reshape(T_PROMPT, D_MODEL), x_sample.reshape(T_SAMPLE, D_MODEL)], 0)
    cond = jnp.concatenate([c_ctx[None, :], c, jnp.zeros((COND_PAD - N_COND, D_MODEL), F32)], 0)
    ret_tables = _rope_tables(RET_DK)
    diff_tables = _rope_tables(DIFF_DH)
    cw, sw = (t.astype(BF16) for t in _dft_tables(LANE))
    cs_p = jnp.concatenate(_dft_tables(SEQ), axis=1).astype(BF16)
    cs_s = jnp.concatenate(_dft_tables(DEC_SEQ), axis=1).astype(BF16)
    conv_w_t = jnp.swapaxes(conv_w, 1, 2)
    w_router_pad = jnp.pad(w_router, ((0, 0), (0, 0), (0, LANE - N_EXPERTS)))
    b_router_pad = jnp.pad(b_router, ((0, 0), (0, LANE - N_EXPERTS))).reshape(DEPTH, 1, LANE)

    mods = [_modulation(cond, w_mod, b_mod, l).reshape(COND_PAD, 6, 1, D_MODEL)
            for l in range(DEPTH)]
    ks_out, vs_out, ss_out = [], [], []
    h = _ln_mod(x, mods[0], 1, 0)
    for l in range(DEPTH):
        lam_init = 0.8 - 0.6 * math.exp(-0.3 * l)
        proj = _matmul(h, w_in, l, F32, IN_TN, "in_proj")
        proj_p = proj[:T_PROMPT]
        ks_out.append(proj_p[:, COL_DK * LANE:COL_DV * LANE]
                      .reshape(BATCH, SEQ, DIFF_HEADS, 2, DIFF_DH))
        vs_out.append(proj_p[:, COL_DV * LANE:COL_CB * LANE]
                      .reshape(BATCH, SEQ, DIFF_HEADS, 2 * DIFF_DH))
        ret_p, s_fin = _retention(proj, ret_decay_logit, l, use_ctx=False)
        ret_s = _retention(proj, ret_decay_logit, l, use_ctx=True, state_ret=state_ret,
                           tables=ret_tables)
        ss_out.append(s_fin)
        diff_p = _diff_attention(proj, diff_lambda, l, lam_init, use_ctx=False)
        diff_s = _diff_attention(proj, diff_lambda, l, lam_init, use_ctx=True, cache_k=cache_k,
                                 cache_v=cache_v, tables=diff_tables)
        conv_p = _gated_conv(proj, conv_w_t, l, use_ctx=False)
        conv_s = _gated_conv(proj, conv_w_t, l, use_ctx=True)
        four_p = _fourier(proj, cw, sw, cs_p, use_ctx=False)
        four_s = _fourier(proj, cw, sw, cs_s, use_ctx=True)
        merged = _merge(h, [ret_p, diff_p, conv_p, four_p], [ret_s, diff_s, conv_s, four_s],
                        w_mgate, b_mgate, w_branch, l)
        mix = _matmul(merged, w_out, l, F32, MM_TN, "out_proj")
        x, h2 = _resid_ln(x, mix, mods[l], 2, ln1_g, ln1_b, l, mods[l], 4, 3, F32)
        idx, wgt, rank, cnt = _router(h2, w_router_pad, b_router_pad, l)
        dest, groups, nvalid, last_tile, padded = _dispatch_plan(idx, rank, cnt)
        xs = _dispatch(h2, dest, last_tile, padded, nvalid)
        y = _experts(xs, groups, nvalid, w_gu, b_gu, w_down, b_down, l)
        nxt = mods[min(l + 1, DEPTH - 1)]
        x, h = _combine_resid_ln(x, y, dest, wgt, mods[l], 5, ln2_g, ln2_b, l, nxt, 1, 0)
    y_p = x[:T_PROMPT].reshape(BATCH, SEQ, D_MODEL)
    y_s = x[T_PROMPT:].reshape(DEC_BATCH, DEC_SEQ, D_MODEL)
    return (y_p, y_s, jnp.stack(ks_out, axis=1), jnp.stack(vs_out, axis=1),
            jnp.stack(ss_out, axis=1))
```
